```python
import jax
import jax.numpy as jnp
from jax import lax
import numpy as np

D_MODEL = 1024
BATCH = 8
SEQ = 4096
DEPTH = 4

GRID_W = 64
CTX_LEN = 256
EPS = 1e-6
HEAD_DIM = 64
ROPE_BASE = 10000.0

HG_HEADS = 4
HG_DK = 64
HG_DV = 64
HG_QK = HG_HEADS * HG_DK
HG_WIDTH = HG_HEADS * HG_DV
HG_CHUNK = 16

ATT_HEADS = 8
ATT_KV_HEADS = 2
ATT_GROUP = ATT_HEADS // ATT_KV_HEADS
ATT_WIDTH = ATT_HEADS * HEAD_DIM
ATT_KV_WIDTH = ATT_KV_HEADS * HEAD_DIM
WINDOW = 128
ATT_BLOCK = 128

LRU_WIDTH = 256
LRU_BLOCKS = 4
LRU_BLOCK = LRU_WIDTH // LRU_BLOCKS
LRU_C = 8.0
LRU_CONV = 4
LRU_CONV_LEFT = 2

D_FF = 2816
FFN_CONV = 3

MIX_WIDTH = HG_WIDTH + ATT_WIDTH + LRU_WIDTH
IN_SPLITS = (HG_QK, HG_QK, HG_QK, HG_WIDTH, HG_WIDTH, ATT_WIDTH, ATT_KV_WIDTH, ATT_KV_WIDTH, LRU_WIDTH, LRU_WIDTH)
IN_WIDTH = sum(IN_SPLITS)

kernel_name = 'hybrid_hgrn2_swa_rglru_dit_block'


def rmsnorm(x, g):
    xf = x.astype(jnp.float32)
    y = xf * lax.rsqrt(jnp.mean(xf * xf, axis=-1, keepdims=True) + EPS)
    return (y * g.astype(jnp.float32)).astype(x.dtype)


def modulate(x, g, shift, scale):
    return rmsnorm(x, g) * (1 + jnp.expand_dims(scale, -2)) + jnp.expand_dims(shift, -2)


def dwconv(x, w, left):
    k, ch = w.shape
    return lax.conv_general_dilated(x, w[:, None, :], window_strides=(1,), padding=[(left, k - 1 - left)],
                                    dimension_numbers=('NWC', 'WIO', 'NWC'), feature_group_count=ch)


def flip_if(t, rev):
    return jnp.flip(t, axis=1) if rev else t


def rope_2d(rows):
    n_freq = HEAD_DIM // 4
    inv = ROPE_BASE ** (-jnp.arange(n_freq, dtype=jnp.float32) / n_freq)
    r = jnp.repeat(jnp.arange(rows, dtype=jnp.float32), GRID_W)
    col = jnp.tile(jnp.arange(GRID_W, dtype=jnp.float32), rows)
    ang = jnp.concatenate([r[:, None] * inv, col[:, None] * inv], axis=-1)
    return jnp.cos(ang), jnp.sin(ang)


def apply_rope(x, cos, sin):
    xf = x.astype(jnp.float32)
    half = HEAD_DIM // 2
    x1, x2 = xf[..., :half], xf[..., half:]
    cs, sn = cos[None, :, None, :], sin[None, :, None, :]
    return jnp.concatenate([x1 * cs - x2 * sn, x2 * cs + x1 * sn], axis=-1).astype(x.dtype)


def hgrn2_gates(z, lb):
    zf = z.astype(jnp.float32)
    log_f = jnp.log(lb + (1.0 - lb) * jax.nn.sigmoid(zf))
    k = (1.0 - lb) * jax.nn.sigmoid(-zf)
    return log_f, k


def hgrn2_chunkwise(q, k, v, log_f, s0):
    b, l, h, _ = q.shape
    n = l // HG_CHUNK

    def chunks(t):
        return t.reshape(b, n, HG_CHUNK, h, t.shape[-1])

    q, k, v, log_f = chunks(q), chunks(k), chunks(v), chunks(log_f)
    a = jnp.cumsum(log_f, axis=2)
    a_end = a[:, :, -1:]
    lower = jnp.tril(jnp.ones((HG_CHUNK, HG_CHUNK), dtype=bool))[:, :, None, None]
    rel = jnp.where(lower, a[:, :, :, None] - a[:, :, None, :], -jnp.inf)
    scores = jnp.einsum('bnthd,bntshd->bnhts', q, jnp.exp(rel) * k[:, :, None])
    o_intra = jnp.einsum('bnhts,bnshe->bnthe', scores, v)
    q_in = q * jnp.exp(a)
    k_out = k * jnp.exp(a_end - a)
    decay = jnp.exp(a_end[:, :, 0])

    def step(s, xs):
        qc, kc, vc, dc = xs
        o_c = jnp.einsum('bthd,bhde->bthe', qc, s)
        s = dc[..., None] * s + jnp.einsum('bthd,bthe->bhde', kc, vc)
        return s, o_c

    xs = (jnp.moveaxis(q_in, 1, 0), jnp.moveaxis(k_out, 1, 0), jnp.moveaxis(v, 1, 0), jnp.moveaxis(decay, 1, 0))
    s_end, o_inter = lax.scan(step, s0, xs)
    o = o_intra + jnp.moveaxis(o_inter, 0, 1)
    return o.reshape(b, l, h, -1), s_end


def hgrn2_final_state(k, v, log_f):
    a = jnp.cumsum(log_f, axis=1)
    return jnp.einsum('blhd,blhe->bhde', k * jnp.exp(a[:, -1:] - a), v)


def hgrn2_mixer(lat, ctx, lb, norm_g, want_ctx):
    lb = lb.reshape(HG_HEADS, HG_DK).astype(jnp.float32)

    def heads(t):
        return t.reshape(t.shape[0], t.shape[1], HG_HEADS, -1).astype(jnp.float32)

    def readout(o, g):
        o = o * lax.rsqrt(jnp.mean(o * o, axis=-1, keepdims=True) + EPS)
        o = o.reshape(o.shape[0], o.shape[1], HG_WIDTH) * norm_g.astype(jnp.float32)
        return (o * jax.nn.silu(g.astype(jnp.float32))).astype(g.dtype)

    q_l, ff_l, fb_l, i_l, g_l = lat
    q_c, ff_c, fb_c, i_c, g_c = ctx
    b = q_l.shape[0]
    q_l = jax.nn.silu(heads(q_l))
    v_l = heads(i_l)
    v_c = heads(i_c)
    if want_ctx:
        q_c = jax.nn.silu(heads(q_c))
    zeros = jnp.zeros((b, HG_HEADS, HG_DK, HG_DV), jnp.float32)
    o_lat = 0.0
    o_ctx = 0.0
    for f_l, f_c, rev in ((ff_l, ff_c, False), (fb_l, fb_c, True)):
        lf_c, k_c = hgrn2_gates(flip_if(heads(f_c), rev), lb)
        if want_ctx:
            oc, s_c = hgrn2_chunkwise(flip_if(q_c, rev), k_c, flip_if(v_c, rev), lf_c, zeros)
            o_ctx = o_ctx + flip_if(oc, rev)
        else:
            s_c = hgrn2_final_state(k_c, flip_if(v_c, rev), lf_c)
        lf_l, k_l = hgrn2_gates(flip_if(heads(f_l), rev), lb)
        ol, _ = hgrn2_chunkwise(flip_if(q_l, rev), k_l, flip_if(v_l, rev), lf_l, s_c)
        o_lat = o_lat + flip_if(ol, rev)
    y_lat = readout(o_lat, g_l)
    y_ctx = readout(o_ctx, g_c) if want_ctx else None
    return y_lat, y_ctx


def window_attention(q_l, k_l, v_l, q_c, k_c, v_c, sink, cos, sin, want_ctx):
    f32 = jnp.float32
    b, l, _ = q_l.shape
    nb = l // ATT_BLOCK
    scale = HEAD_DIM ** -0.5
    q_l = apply_rope(q_l.reshape(b, l, ATT_HEADS, HEAD_DIM), cos, sin)
    k_l = apply_rope(k_l.reshape(b, l, ATT_KV_HEADS, HEAD_DIM), cos, sin)
    v_l = v_l.reshape(b, l, ATT_KV_HEADS, HEAD_DIM)
    k_c = k_c.reshape(b, -1, ATT_KV_HEADS, HEAD_DIM)
    v_c = v_c.reshape(b, -1, ATT_KV_HEADS, HEAD_DIM)
    sink = sink.astype(f32).reshape(ATT_KV_HEADS, ATT_GROUP, 1, 1)

    qb = q_l.reshape(b, nb, ATT_BLOCK, ATT_KV_HEADS, ATT_GROUP, HEAD_DIM)

    def band(t):
        tp = jnp.pad(t, ((0, 0), (ATT_BLOCK, ATT_BLOCK), (0, 0), (0, 0)))
        tp = tp.reshape(b, nb + 2, ATT_BLOCK, ATT_KV_HEADS, HEAD_DIM)
        return jnp.concatenate([tp[:, :-2], tp[:, 1:-1], tp[:, 2:]], axis=2)

    kw, vw = band(k_l), band(v_l)
    q_pos = jnp.arange(l).reshape(nb, ATT_BLOCK)
    k_pos = (jnp.arange(nb)[:, None] - 1) * ATT_BLOCK + jnp.arange(3 * ATT_BLOCK)[None, :]
    valid = ((jnp.abs(q_pos[:, :, None] - k_pos[:, None, :]) <= WINDOW)
             & (k_pos >= 0)[:, None, :] & (k_pos < l)[:, None, :])
    s_loc = jnp.einsum('bnqhgd,bnkhd->bnhgqk', qb, kw, preferred_element_type=f32) * scale
    s_loc = jnp.where(valid[None, :, None, None], s_loc, -jnp.inf)
    s_ctx = jnp.einsum('bnqhgd,bchd->bnhgqc', qb, k_c, preferred_element_type=f32) * scale
    m = jnp.maximum(jnp.maximum(s_loc.max(-1, keepdims=True), s_ctx.max(-1, keepdims=True)), sink)
    e_loc = jnp.exp(s_loc - m)
    e_ctx = jnp.exp(s_ctx - m)
    denom = e_loc.sum(-1) + e_ctx.sum(-1) + jnp.exp(sink - m)[..., 0]
    o = (jnp.einsum('bnhgqk,bnkhd->bnhgqd', e_loc, vw.astype(f32))
         + jnp.einsum('bnhgqc,bchd->bnhgqd', e_ctx, v_c.astype(f32)))
    o = o / denom[..., None]
    y_lat = jnp.transpose(o, (0, 1, 4, 2, 3, 5)).reshape(b, l, ATT_WIDTH).astype(q_l.dtype)

    y_ctx = None
    if want_ctx:
        qc = q_c.reshape(b, -1, ATT_KV_HEADS, ATT_GROUP, HEAD_DIM)
        s = jnp.einsum('bqhgd,bkhd->bhgqk', qc, k_c, preferred_element_type=f32) * scale
        mc = jnp.maximum(s.max(-1, keepdims=True), sink)
        e = jnp.exp(s - mc)
        oc = jnp.einsum('bhgqk,bkhd->bhgqd', e, v_c.astype(f32))
        oc = oc / (e.sum(-1) + jnp.exp(sink - mc)[..., 0])[..., None]
        y_ctx = jnp.transpose(oc, (0, 3, 1, 2, 4)).reshape(b, -1, ATT_WIDTH).astype(q_c.dtype)
    return y_lat, y_ctx


def rglru_coeffs(x, w_r, b_r, w_i, b_i, lam):
    f32 = jnp.float32
    xb = x.reshape(x.shape[0], x.shape[1], LRU_BLOCKS, LRU_BLOCK)
    r = jax.nn.sigmoid(jnp.einsum('blnd,nde->blne', xb, w_r.astype(f32)).reshape(x.shape) + b_r.astype(f32))
    i = jax.nn.sigmoid(jnp.einsum('blnd,nde->blne', xb, w_i.astype(f32)).reshape(x.shape) + b_i.astype(f32))
    log_a = -LRU_C * r * jax.nn.softplus(-lam.astype(f32))
    u = jnp.sqrt(-jnp.expm1(2.0 * log_a)) * (i * x)
    return log_a, u


def linear_scan(log_a, u, h0):
    def combine(left, right):
        return left[0] * right[0], right[0] * left[1] + right[1]

    a_cum, h = lax.associative_scan(combine, (jnp.exp(log_a), u), axis=1)
    return h + a_cum * h0[:, None]


def lru_final_state(log_a, u):
    cl = jnp.cumsum(log_a, axis=1)
    return jnp.sum(jnp.exp(cl[:, -1:] - cl) * u, axis=1)


def rglru_mixer(x_l, y_l, x_c, y_c, conv_w, conv_b, w_r, b_r, w_i, b_i, lam, want_ctx):
    xl = (dwconv(x_l, conv_w, LRU_CONV_LEFT) + conv_b).astype(jnp.float32)
    xc = (dwconv(x_c, conv_w, LRU_CONV_LEFT) + conv_b).astype(jnp.float32)
    zeros = jnp.zeros((xl.shape[0], LRU_WIDTH), jnp.float32)
    h_l = 0.0
    h_c = 0.0
    for d in range(2):
        rev = d == 1
        la_c, u_c = rglru_coeffs(flip_if(xc, rev), w_r[d], b_r[d], w_i[d], b_i[d], lam[d])
        if want_ctx:
            hc = linear_scan(la_c, u_c, zeros)
            s_c = hc[:, -1]
            h_c = h_c + flip_if(hc, rev)
        else:
            s_c = lru_final_state(la_c, u_c)
        la_l, u_l = rglru_coeffs(flip_if(xl, rev), w_r[d], b_r[d], w_i[d], b_i[d], lam[d])
        h_l = h_l + flip_if(linear_scan(la_l, u_l, s_c), rev)
    y_lat = (jax.nn.gelu(y_l.astype(jnp.float32)) * h_l).astype(y_l.dtype)
    y_ctx = (jax.nn.gelu(y_c.astype(jnp.float32)) * h_c).astype(y_c.dtype) if want_ctx else None
    return y_lat, y_ctx


def split_proj(p):
    idx = np.cumsum(IN_SPLITS)[:-1].tolist()
    return jnp.split(p, idx, axis=-1)


def token_mixer(h_l, h_c, w_in, w_out, lb, hg_norm_g, sink, conv_w, conv_b, w_r, b_r, w_i, b_i, lam,
                cos, sin, want_ctx):
    pl = split_proj(h_l @ w_in)
    pc = split_proj(h_c @ w_in)
    hg_l, hg_c = hgrn2_mixer(pl[0:5], pc[0:5], lb, hg_norm_g, want_ctx)
    at_l, at_c = window_attention(pl[5], pl[6], pl[7], pc[5], pc[6], pc[7], sink, cos, sin, want_ctx)
    lr_l, lr_c = rglru_mixer(pl[8], pl[9], pc[8], pc[9], conv_w, conv_b, w_r, b_r, w_i, b_i, lam, want_ctx)
    y_l = jnp.concatenate([hg_l, at_l, lr_l], axis=-1) @ w_out
    y_c = jnp.concatenate([hg_c, at_c, lr_c], axis=-1) @ w_out if want_ctx else None
    return y_l, y_c


def conv_ffn(h, w_up, conv_w, conv_b, w_down):
    u = dwconv(h @ w_up, conv_w, FFN_CONV // 2) + conv_b
    gate, val = jnp.split(u, 2, axis=-1)
    return (jax.nn.silu(gate) * val) @ w_down


def setup_inputs(seed: int = 0) -> dict:
    key = jax.random.key(seed)
    ks = jax.random.split(key, 26)
    f32 = jnp.float32

    def nrm(k, shape, scale):
        return jax.random.normal(k, shape, f32) * scale

    d = D_MODEL
    u = jax.random.uniform(ks[18], (DEPTH, 2, LRU_WIDTH), f32, 0.9, 0.999)
    a_base = u ** (1.0 / LRU_C)
    return {
        'x': nrm(ks[0], (BATCH, SEQ, d), 1.0),
        'c': nrm(ks[1], (BATCH, d), 1.0),
        'ctx': nrm(ks[2], (BATCH, CTX_LEN, d), 1.0),
        'c_ctx': nrm(ks[3], (d,), 1.0),
        'ada_w': nrm(ks[4], (DEPTH, d, 6 * d), 0.5 * d ** -0.5),
        'ada_b': nrm(ks[5], (DEPTH, 6 * d), 0.01),
        'norm_mix_g': 1.0 + nrm(ks[6], (DEPTH, d), 0.1),
        'norm_ffn_g': 1.0 + nrm(ks[7], (DEPTH, d), 0.1),
        'w_in': nrm(ks[8], (DEPTH, d, IN_WIDTH), d ** -0.5),
        'hg_lb_raw': 1.0 + nrm(ks[9], (DEPTH, HG_QK), 0.5),
        'hg_norm_g': 1.0 + nrm(ks[10], (DEPTH, HG_WIDTH), 0.1),
        'att_sink': nrm(ks[11], (DEPTH, ATT_HEADS), 0.5),
        'lru_conv_w': nrm(ks[12], (DEPTH, LRU_CONV, LRU_WIDTH), LRU_CONV ** -0.5),
        'lru_conv_b': nrm(ks[13], (DEPTH, LRU_WIDTH), 0.01),
        'lru_w_r': nrm(ks[14], (DEPTH, 2, LRU_BLOCKS, LRU_BLOCK, LRU_BLOCK), LRU_BLOCK ** -0.5),
        'lru_b_r': nrm(ks[15], (DEPTH, 2, LRU_WIDTH), 0.01),
        'lru_w_i': nrm(ks[16], (DEPTH, 2, LRU_BLOCKS, LRU_BLOCK, LRU_BLOCK), LRU_BLOCK ** -0.5),
        'lru_b_i': nrm(ks[17], (DEPTH, 2, LRU_WIDTH), 0.01),
        'lru_lambda': jnp.log(a_base) - jnp.log1p(-a_base),
        'w_out': nrm(ks[19], (DEPTH, MIX_WIDTH, d), MIX_WIDTH ** -0.5),
        'ffn_w_up': nrm(ks[20], (DEPTH, d, 2 * D_FF), d ** -0.5),
        'ffn_conv_w': nrm(ks[21], (DEPTH, FFN_CONV, 2 * D_FF), FFN_CONV ** -0.5),
        'ffn_conv_b': nrm(ks[22], (DEPTH, 2 * D_FF), 0.01),
        'ffn_w_down': nrm(ks[23], (DEPTH, D_FF, d), D_FF ** -0.5),
        'final_norm_g': 1.0 + nrm(ks[24], (d,), 0.1),
    }


def reference(x, c, ctx, c_ctx, ada_w, ada_b, norm_mix_g, norm_ffn_g, w_in, hg_lb_raw, hg_norm_g,
              att_sink, lru_conv_w, lru_conv_b, lru_w_r, lru_b_r, lru_w_i, lru_b_i, lru_lambda,
              w_out, ffn_w_up, ffn_conv_w, ffn_conv_b, ffn_w_down, final_norm_g):
    seq_len = x.shape[1]
    rows = seq_len // GRID_W
    cos, sin = rope_2d(rows)
    p = jax.nn.softmax(hg_lb_raw.astype(jnp.float32), axis=0)
    lbs = jnp.cumsum(p, axis=0) - p[0]
    sc = jax.nn.silu(c)
    scc = jax.nn.silu(c_ctx)
    h_ctx = ctx
    for layer in range(DEPTH):
        want_ctx = layer < DEPTH - 1
        mod = sc @ ada_w[layer] + ada_b[layer]
        mod_c = scc @ ada_w[layer] + ada_b[layer]
        sh1, sc1, g1, sh2, sc2, g2 = jnp.split(mod, 6, axis=-1)
        sh1c, sc1c, g1c, sh2c, sc2c, g2c = jnp.split(mod_c, 6, axis=-1)
        h_l = modulate(x, norm_mix_g[layer], sh1, sc1)
        h_c = modulate(h_ctx, norm_mix_g[layer], sh1c, sc1c)
        y_l, y_c = token_mixer(h_l, h_c, w_in[layer], w_out[layer], lbs[layer], hg_norm_g[layer],
                               att_sink[layer], lru_conv_w[layer], lru_conv_b[layer], lru_w_r[layer],
                               lru_b_r[layer], lru_w_i[layer], lru_b_i[layer], lru_lambda[layer],
                               cos, sin, want_ctx)
        x = x + g1[:, None, :] * y_l
        x = x + g2[:, None, :] * conv_ffn(modulate(x, norm_ffn_g[layer], sh2, sc2), ffn_w_up[layer],
                                          ffn_conv_w[layer], ffn_conv_b[layer], ffn_w_down[layer])
        if want_ctx:
            h_ctx = h_ctx + g1c * y_c
            h_ctx = h_ctx + g2c * conv_ffn(modulate(h_ctx, norm_ffn_g[layer], sh2c, sc2c), ffn_w_up[layer],
                                           ffn_conv_w[layer], ffn_conv_b[layer], ffn_w_down[layer])
    return rmsnorm(x, final_norm_g)
```

```python
import functools

import jax
import jax.numpy as jnp
from jax import lax
from jax.experimental import pallas as pl
from jax.experimental.pallas import tpu as pltpu

F32 = jnp.float32
BF16 = jnp.bfloat16
ACT = BF16

EPS = 1e-6
GRID_W = 64
HEAD_DIM = 64
ROPE_BASE = 10000.0

HG_HEADS = 4
HG_DK = 64
HG_W = HG_HEADS * HG_DK
HG_CHUNK = 16
HG_TILE = 128

ATT_HEADS = 8
ATT_KV_HEADS = 2
ATT_GROUP = ATT_HEADS // ATT_KV_HEADS
ATT_W = ATT_HEADS * HEAD_DIM
ATT_KV_W = ATT_KV_HEADS * HEAD_DIM
WINDOW = 128
ATT_BLOCK = 128
ATT_SLAB = ATT_GROUP * HEAD_DIM

LRU_W = 256
LRU_BLOCKS = 4
LRU_C = 8.0
LRU_CONV = 4
LRU_CONV_LEFT = 2
LRU_TILE = 128

FFN_CONV = 3
FFN_CHUNK = 256
HALO = 16

IN_HG = 5 * HG_W
IN_W = IN_HG + ATT_W + 2 * ATT_KV_W + 2 * LRU_W

VMEM_LIMIT = 52 * 1024 * 1024


def _cparams(*sem):
    return pltpu.CompilerParams(dimension_semantics=sem, vmem_limit_bytes=VMEM_LIMIT)


def _silu(x):
    return x * jax.nn.sigmoid(x)


def _mod_kernel(c_ref, w_ref, b_ref, o_ref):
    s = _silu(c_ref[...])
    o_ref[...] = jnp.dot(s, w_ref[...], precision=lax.Precision.HIGHEST,
                         preferred_element_type=F32) + b_ref[...]


def _modulation(crows, ada_w, ada_b):
    depth, d, n = ada_w.shape
    rows = crows.shape[0]
    tn = 1536
    return pl.pallas_call(
        _mod_kernel,
        out_shape=jax.ShapeDtypeStruct((depth, rows, n), F32),
        grid=(depth, n // tn),
        in_specs=[pl.BlockSpec((rows, d), lambda l, j: (0, 0)),
                  pl.BlockSpec((None, d, tn), lambda l, j: (l, 0, j)),
                  pl.BlockSpec((None, 1, tn), lambda l, j: (l, 0, j))],
        out_specs=pl.BlockSpec((None, rows, tn), lambda l, j: (l, 0, j)),
        compiler_params=_cparams("parallel", "parallel"),
        name="modulation",
    )(crows, ada_w, ada_b.reshape(depth, 1, n))


def _lb_kernel(raw_ref, o_ref):
    raw = raw_ref[...]
    e = jnp.exp(raw - jnp.max(raw, axis=0, keepdims=True))
    p = e / jnp.sum(e, axis=0, keepdims=True)
    depth = raw.shape[0]
    acc = jnp.zeros_like(p[0:1])
    for l in range(depth):
        acc = acc + p[l:l + 1]
        o_ref[l:l + 1, :] = acc - p[0:1]


def _lower_bounds(raw):
    return pl.pallas_call(_lb_kernel, out_shape=jax.ShapeDtypeStruct(raw.shape, F32),
                          name="hg_lower_bounds")(raw.astype(F32))


def _norm_mod(x, g, shift, scale):
    y = x * lax.rsqrt(jnp.mean(x * x, axis=-1, keepdims=True) + EPS)
    return (y * g) * (1.0 + scale) + shift


def _rot_half(x):
    w = x.shape[-1]
    lane = lax.broadcasted_iota(jnp.int32, x.shape, 1) & (HEAD_DIM - 1)
    return jnp.where(lane < HEAD_DIM // 2, pltpu.roll(x, w - HEAD_DIM // 2, 1),
                     pltpu.roll(x, HEAD_DIM // 2, 1))


def _inproj_kernel(x_ref, g_ref, sh_ref, sc_ref, w_ref, cos_ref, sin_ref,
                   hg_ref, q_ref, k_ref, v_ref, lru_ref, *, rope):
    h = _norm_mod(x_ref[...], g_ref[...], sh_ref[...], sc_ref[...])
    p = jnp.dot(h.astype(BF16), w_ref[...], preferred_element_type=F32)
    o = IN_HG
    hg_ref[...] = p[:, :o].astype(ACT)
    q = p[:, o:o + ATT_W] * (HEAD_DIM ** -0.5)
    k = p[:, o + ATT_W:o + ATT_W + ATT_KV_W]
    if rope:
        cos, sin = cos_ref[...], sin_ref[...]
        cos_q = jnp.concatenate([cos] * (ATT_W // ATT_KV_W), axis=1)
        sin_q = jnp.concatenate([sin] * (ATT_W // ATT_KV_W), axis=1)
        q = q * cos_q + _rot_half(q) * sin_q
        k = k * cos + _rot_half(k) * sin
    q_ref[...] = q.astype(ACT)
    k_ref[...] = k.astype(ACT)
    o += ATT_W + ATT_KV_W
    v_ref[...] = p[:, o:o + ATT_KV_W].astype(ACT)
    o += ATT_KV_W
    lru_ref[...] = p[:, o:].astype(ACT)


def _bmap(nb_arr, b_total):
    return (lambda b, i: (b, 0, 0)) if nb_arr == b_total else (lambda b, i: (0, 0, 0))


def _inproj(x, g, shift, scale, w_in, cos, sin, *, rope, tm):
    b, l, d = x.shape
    tm = min(tm, l)
    row = lambda w: pl.BlockSpec((None, tm, w), lambda bb, i: (bb, i, 0))
    par = lambda a: pl.BlockSpec((None, 1, d), _bmap(a.shape[0], b))
    tab = pl.BlockSpec((tm, ATT_KV_W), lambda bb, i: (i, 0))
    widths = (IN_HG, ATT_W, ATT_KV_W, ATT_KV_W, 2 * LRU_W)
    return pl.pallas_call(
        functools.partial(_inproj_kernel, rope=rope),
        out_shape=[jax.ShapeDtypeStruct((b, l, w), ACT) for w in widths],
        grid=(b, l // tm),
        in_specs=[row(d), pl.BlockSpec((1, d), lambda bb, i: (0, 0)), par(shift), par(scale),
                  pl.BlockSpec(w_in.shape, lambda bb, i: (0, 0)), tab, tab],
        out_specs=[row(w) for w in widths],
        compiler_params=_cparams("parallel", "parallel"),
        name="inproj_rope" if rope else "inproj_ctx",
    )(x, g, shift, scale, w_in, cos, sin)


def _hg_tile(qr, zr, vr, st, lb, blk, blkmask, rev, emit):
    t, w = zr.shape
    nc = t // HG_CHUNK
    z = zr.astype(F32)
    v = vr.astype(F32)
    f = lb + (1.0 - lb) * jax.nn.sigmoid(z)
    lf = jnp.log(f)
    kk = (1.0 - lb) * jax.nn.sigmoid(-z)
    row = lax.broadcasted_iota(jnp.int32, (t, w), 0) & (HG_CHUNK - 1)
    cum = lf
    sft = 1
    while sft < HG_CHUNK:
        cum = cum + jnp.where(row >= sft, pltpu.roll(cum, sft, 0), 0.0)
        sft *= 2
    cum3 = cum.reshape(nc, HG_CHUNK, w)
    tot3 = cum3[:, HG_CHUNK - 1:HG_CHUNK, :]
    if rev:
        rel_base = (lf - cum).reshape(nc, HG_CHUNK, w)
        e_q = tot3 + rel_base
        e_k = -rel_base
    else:
        rel_base = cum3
        e_q = cum3
        e_k = tot3 - cum3
    k_out = (kk.reshape(nc, HG_CHUNK, w) * jnp.exp(e_k)).astype(BF16)
    dec = jnp.exp(tot3)
    v3 = v.reshape(nc, HG_CHUNK, w)
    vb = v3.astype(BF16)

    o3 = None
    if emit:
        q = _silu(qr.astype(F32))
        q3 = q.reshape(nc, HG_CHUNK, w)
        k3 = kk.reshape(nc, HG_CHUNK, w)
        row3 = row.reshape(nc, HG_CHUNK, w)
        q_in = (q3 * jnp.exp(e_q)).astype(BF16)
        o3 = jnp.zeros((nc, HG_CHUNK, w), F32)
        for s in range(HG_CHUNK):
            rel = rel_base - rel_base[:, s:s + 1, :]
            valid = (row3 <= s) if rev else (row3 >= s)
            wgt = q3 * jnp.where(valid, jnp.exp(rel), 0.0) * k3[:, s:s + 1, :]
            sc = jnp.dot(wgt.reshape(t, w).astype(BF16), blk, preferred_element_type=F32)
            o3 = o3 + sc.reshape(nc, HG_CHUNK, w) * v3[:, s:s + 1, :]

    outs = [None] * nc
    order = range(nc - 1, -1, -1) if rev else range(nc)
    for c in order:
        if emit:
            o_int = lax.dot_general(q_in[c], st.astype(BF16), (((1,), (1,)), ((), ())),
                                    preferred_element_type=F32)
            outs[c] = o3[c] + o_int
        kv = lax.dot_general(vb[c], k_out[c], (((0,), (0,)), ((), ())),
                             preferred_element_type=F32)
        st = st * dec[c] + kv * blkmask
    o = jnp.concatenate(outs, axis=0) if emit else None
    return o, st


def _hg_seq(p_ref, o_ref, st_ref, lb, blk, blkmask, emit):
    n = p_ref.shape[0] // HG_TILE
    if emit:
        o_ref[...] = jnp.zeros(o_ref.shape, F32)

    def body(j, carry):
        for d, rev in ((0, False), (1, True)):
            jt = (n - 1 - j) if rev else j
            r0 = pl.multiple_of(jt * HG_TILE, HG_TILE)
            rows = pl.ds(r0, HG_TILE)
            qr = p_ref[rows, 0:HG_W]
            zr = p_ref[rows, (1 + d) * HG_W:(2 + d) * HG_W]
            vr = p_ref[rows, 3 * HG_W:4 * HG_W]
            o, st = _hg_tile(qr, zr, vr, st_ref[d], lb, blk, blkmask, rev, emit)
            st_ref[d] = st
            if emit:
                o_ref[rows, :] += o
        return carry

    lax.fori_loop(0, n, body, 0)


def _hg_readout(p_ref, o_ref, y_ref, ng, blk):
    n = p_ref.shape[0] // HG_TILE

    def body(j, carry):
        rows = pl.ds(pl.multiple_of(j * HG_TILE, HG_TILE), HG_TILE)
        o = o_ref[rows, :]
        sq = o * o
        hi = sq.astype(BF16)
        lo = (sq - hi.astype(F32)).astype(BF16)
        ms = (jnp.dot(hi, blk, preferred_element_type=F32)
              + jnp.dot(lo, blk, preferred_element_type=F32)) * (1.0 / HG_DK)
        g = p_ref[rows, 4 * HG_W:5 * HG_W].astype(F32)
        y_ref[rows, :] = (o * lax.rsqrt(ms + EPS) * ng * _silu(g)).astype(y_ref.dtype)
        return carry

    lax.fori_loop(0, n, body, 0)


def _hg_kernel(*refs, want_ctx):
    if want_ctx:
        pl_ref, pc_ref, lb_ref, ng_ref, blk_ref, yl_ref, yc_ref, ol_ref, oc_ref, st_ref = refs
    else:
        pl_ref, pc_ref, lb_ref, ng_ref, blk_ref, yl_ref, ol_ref, st_ref = refs
        yc_ref = oc_ref = None
    lb = lb_ref[...]
    ng = ng_ref[...]
    blk = blk_ref[...]
    blkmask = blk.astype(F32)
    st_ref[...] = jnp.zeros(st_ref.shape, F32)
    _hg_seq(pc_ref, oc_ref, st_ref, lb, blk, blkmask, want_ctx)
    _hg_seq(pl_ref, ol_ref, st_ref, lb, blk, blkmask, True)
    _hg_readout(pl_ref, ol_ref, yl_ref, ng, blk)
    if want_ctx:
        _hg_readout(pc_ref, oc_ref, yc_ref, ng, blk)


def _hgrn2(p_l, p_c, lb, ng, blk, *, want_ctx):
    b, l, _ = p_l.shape
    c = p_c.shape[1]
    seq = lambda n, w: pl.BlockSpec((None, n, w), lambda bb: (bb, 0, 0))
    vec = pl.BlockSpec((1, HG_W), lambda bb: (0, 0))
    out_shape = [jax.ShapeDtypeStruct((b, l, HG_W), ACT)]
    out_specs = [seq(l, HG_W)]
    scratch = [pltpu.VMEM((l, HG_W), F32)]
    if want_ctx:
        out_shape.append(jax.ShapeDtypeStruct((b, c, HG_W), ACT))
        out_specs.append(seq(c, HG_W))
        scratch.append(pltpu.VMEM((c, HG_W), F32))
    scratch.append(pltpu.VMEM((2, HG_W, HG_W), F32))
    res = pl.pallas_call(
        functools.partial(_hg_kernel, want_ctx=want_ctx),
        out_shape=out_shape,
        grid=(b,),
        in_specs=[seq(l, IN_HG), seq(c, IN_HG), vec, vec,
                  pl.BlockSpec((HG_W, HG_W), lambda bb: (0, 0))],
        out_specs=out_specs,
        scratch_shapes=scratch,
        compiler_params=_cparams("parallel"),
        name="hgrn2_mixer",
    )(p_l, p_c, lb, ng, blk)
    return (res[0], res[1]) if want_ctx else (res[0], None)


def _attn_kernel(*refs, local, n_ctx, seq_len):
    if local:
        (sink_ref, q_ref, k_ref, v_ref, kc_ref, vc_ref, rep_ref, o_ref,
         k4_ref, v4_ref, kc4_ref, vc4_ref) = refs
    else:
        sink_ref, q_ref, kc_ref, vc_ref, rep_ref, o_ref, kc4_ref, vc4_ref = refs
    n = pl.program_id(1)
    span = 3 * ATT_BLOCK

    @pl.when(n == 0)
    def _():
        rep = rep_ref[...]

        def expand(src, dst, rows):
            step = min(rows, 512)

            def body(i, carry):
                r = pl.ds(pl.multiple_of(i * step, step), step)
                dst[r, :] = jnp.dot(src[r, :], rep, preferred_element_type=F32).astype(BF16)
                return carry

            lax.fori_loop(0, rows // step, body, 0)

        expand(kc_ref, kc4_ref, n_ctx)
        expand(vc_ref, vc4_ref, n_ctx)
        if local:
            expand(k_ref, k4_ref, seq_len)
            expand(v_ref, v4_ref, seq_len)

    rows = ATT_GROUP * ATT_BLOCK
    lane_head = lax.broadcasted_iota(jnp.int32, (ATT_BLOCK, ATT_SLAB), 1) // HEAD_DIM
    row_head = lax.broadcasted_iota(jnp.int32, (rows, 1), 0) // ATT_BLOCK
    if local:
        start = jnp.clip((n - 1) * ATT_BLOCK, 0, seq_len - span)
        start = pl.multiple_of(start, ATT_BLOCK)
        q_pos = n * ATT_BLOCK + (lax.broadcasted_iota(jnp.int32, (rows, span), 0) & (ATT_BLOCK - 1))
        k_pos = start + lax.broadcasted_iota(jnp.int32, (rows, span), 1)
        valid = jnp.abs(q_pos - k_pos) <= WINDOW

    nt = (((1,), (1,)), ((), ()))
    for h in range(ATT_KV_HEADS):
        cols = slice(h * ATT_SLAB, (h + 1) * ATT_SLAB)
        qs = q_ref[:, cols]
        zero = jnp.zeros_like(qs)
        lhs = jnp.concatenate([jnp.where(lane_head == g, qs, zero) for g in range(ATT_GROUP)], axis=0)
        sink = jnp.zeros((rows, 1), F32)
        for g in range(ATT_GROUP):
            sink = jnp.where(row_head == g, sink_ref[0, h * ATT_GROUP + g], sink)
        s_ctx = lax.dot_general(lhs, kc4_ref[:, cols], nt, preferred_element_type=F32)
        m = jnp.maximum(jnp.max(s_ctx, axis=-1, keepdims=True), sink)
        if local:
            s_loc = lax.dot_general(lhs, k4_ref[pl.ds(start, span), cols], nt, preferred_element_type=F32)
            s_loc = jnp.where(valid, s_loc, -jnp.inf)
            m = jnp.maximum(m, jnp.max(s_loc, axis=-1, keepdims=True))
        e_ctx = jnp.exp(s_ctx - m)
        den = jnp.sum(e_ctx, axis=-1, keepdims=True) + jnp.exp(sink - m)
        o = jnp.dot(e_ctx.astype(BF16), vc4_ref[:, cols], preferred_element_type=F32)
        if local:
            e_loc = jnp.exp(s_loc - m)
            den = den + jnp.sum(e_loc, axis=-1, keepdims=True)
            o = o + jnp.dot(e_loc.astype(BF16), v4_ref[pl.ds(start, span), cols], preferred_element_type=F32)
        o = o / den
        y = jnp.zeros((ATT_BLOCK, ATT_SLAB), F32)
        for g in range(ATT_GROUP):
            y = jnp.where(lane_head == g, o[g * ATT_BLOCK:(g + 1) * ATT_BLOCK], y)
        o_ref[:, cols] = y.astype(o_ref.dtype)


def _attention(sink, q, k, v, kc, vc, rep, *, local):
    b, l, _ = q.shape
    c = kc.shape[1]
    nb = l // ATT_BLOCK
    whole = lambda n: pl.BlockSpec((None, n, ATT_KV_W), lambda bb, i: (bb, 0, 0))
    qspec = pl.BlockSpec((None, ATT_BLOCK, ATT_W), lambda bb, i: (bb, i, 0))
    smem = pl.BlockSpec(memory_space=pltpu.SMEM)
    repspec = pl.BlockSpec(rep.shape, lambda bb, i: (0, 0))
    wide = ATT_KV_HEADS * ATT_SLAB
    if local:
        args = (sink, q, k, v, kc, vc, rep)
        in_specs = [smem, qspec, whole(l), whole(l), whole(c), whole(c), repspec]
        scratch = [pltpu.VMEM((l, wide), BF16), pltpu.VMEM((l, wide), BF16),
                   pltpu.VMEM((c, wide), BF16), pltpu.VMEM((c, wide), BF16)]
    else:
        args = (sink, q, kc, vc, rep)
        in_specs = [smem, qspec, whole(c), whole(c), repspec]
        scratch = [pltpu.VMEM((c, wide), BF16), pltpu.VMEM((c, wide), BF16)]
    return pl.pallas_call(
        functools.partial(_attn_kernel, local=local, n_ctx=c, seq_len=l),
        out_shape=jax.ShapeDtypeStruct((b, l, ATT_W), ACT),
        grid=(b, nb),
        in_specs=in_specs,
        out_specs=qspec,
        scratch_shapes=scratch,
        compiler_params=_cparams("parallel", "arbitrary"),
        name="window_attention" if local else "context_attention",
    )(*args)


def _lru_scan_tile(a, u, carry, rev):
    t, w = a.shape
    row = lax.broadcasted_iota(jnp.int32, (t, w), 0)
    sft = 1
    while sft < t:
        if rev:
            ok = row < t - sft
            a_s = pltpu.roll(a, t - sft, 0)
            u_s = pltpu.roll(u, t - sft, 0)
        else:
            ok = row >= sft
            a_s = pltpu.roll(a, sft, 0)
            u_s = pltpu.roll(u, sft, 0)
        u = u + a * jnp.where(ok, u_s, 0.0)
        a = a * jnp.where(ok, a_s, 1.0)
        sft *= 2
    h = u + a * carry
    new_carry = h[0:1, :] if rev else h[t - 1:t, :]
    return h, new_carry


def _lru_seq(p_ref, y_ref, xpad_ref, hsum_ref, carries, prm, emit):
    n_rows = p_ref.shape[0]
    t = min(LRU_TILE, n_rows)
    n = n_rows // t
    cw, cb, wr, br, wi, bi, sp = prm
    pad = 8
    zeros = jnp.zeros((pad, LRU_W), F32)
    xpad_ref[0:pad, :] = zeros
    xpad_ref[pad + n_rows:2 * pad + n_rows, :] = zeros

    def fill(j, carry):
        r0 = pl.multiple_of(j * t, t)
        xpad_ref[pl.ds(r0 + pad, t), :] = p_ref[pl.ds(r0, t), 0:LRU_W].astype(F32)
        if emit:
            hsum_ref[pl.ds(r0, t), :] = jnp.zeros((t, LRU_W), F32)
        return carry

    lax.fori_loop(0, n, fill, 0)
    win_rows = t + 2 * pad

    def body(j, carry):
        new = []
        for d in range(2):
            rev = d == 1
            jt = (n - 1 - j) if rev else j
            r0 = pl.multiple_of(jt * t, t)
            win = xpad_ref[pl.ds(r0, win_rows), :]
            xc = cb
            for k in range(LRU_CONV):
                shift = (LRU_CONV_LEFT - k) % win_rows
                sh = win if shift == 0 else pltpu.roll(win, shift, 0)
                xc = xc + cw[k:k + 1, :] * sh[pad:pad + t, :]
            xb = xc.astype(BF16)
            r = jax.nn.sigmoid(jnp.dot(xb, wr[d], preferred_element_type=F32) + br[d:d + 1, :])
            i = jax.nn.sigmoid(jnp.dot(xb, wi[d], preferred_element_type=F32) + bi[d:d + 1, :])
            log_a = (-LRU_C) * r * sp[d:d + 1, :]
            a = jnp.exp(log_a)
            u = jnp.sqrt(-jnp.tanh(log_a) * (a * a + 1.0)) * (i * xc)
            h, c_new = _lru_scan_tile(a, u, carry[d], rev)
            new.append(c_new)
            if emit:
                hsum_ref[pl.ds(r0, t), :] += h
        return tuple(new)

    carries = lax.fori_loop(0, n, body, carries)

    if emit:
        def out(j, carry):
            rows = pl.ds(pl.multiple_of(j * t, t), t)
            yv = p_ref[rows, LRU_W:2 * LRU_W].astype(F32)
            y_ref[rows, :] = (jax.nn.gelu(yv) * hsum_ref[rows, :]).astype(y_ref.dtype)
            return carry

        lax.fori_loop(0, n, out, 0)
    return carries


def _lru_kernel(*refs, want_ctx):
    if want_ctx:
        (pl_ref, pc_ref, cw_ref, cb_ref, wr_ref, br_ref, wi_ref, bi_ref, lam_ref,
         yl_ref, yc_ref, xpad_ref, hl_ref, hc_ref) = refs
    else:
        (pl_ref, pc_ref, cw_ref, cb_ref, wr_ref, br_ref, wi_ref, bi_ref, lam_ref,
         yl_ref, xpad_ref, hl_ref) = refs
        yc_ref = hc_ref = None
    sp = jax.nn.softplus(-lam_ref[...])
    prm = (cw_ref[...], cb_ref[...], (wr_ref[0], wr_ref[1]), br_ref[...],
           (wi_ref[0], wi_ref[1]), bi_ref[...], sp)
    zero = jnp.zeros((1, LRU_W), F32)
    carries = _lru_seq(pc_ref, yc_ref, xpad_ref, hc_ref, (zero, zero), prm, want_ctx)
    _lru_seq(pl_ref, yl_ref, xpad_ref, hl_ref, carries, prm, True)


def _rglru(p_l, p_c, cw, cb, wr, br, wi, bi, lam, *, want_ctx):
    b, l, _ = p_l.shape
    c = p_c.shape[1]
    seq = lambda n, w: pl.BlockSpec((None, n, w), lambda bb: (bb, 0, 0))
    full = lambda a: pl.BlockSpec(a.shape, lambda bb: (0,) * a.ndim)
    out_shape = [jax.ShapeDtypeStruct((b, l, LRU_W), ACT)]
    out_specs = [seq(l, LRU_W)]
    scratch = [pltpu.VMEM((l + 16, LRU_W), F32), pltpu.VMEM((l, LRU_W), F32)]
    if want_ctx:
        out_shape.append(jax.ShapeDtypeStruct((b, c, LRU_W), ACT))
        out_specs.append(seq(c, LRU_W))
        scratch.append(pltpu.VMEM((c, LRU_W), F32))
    params = (cw, cb, wr, br, wi, bi, lam)
    res = pl.pallas_call(
        functools.partial(_lru_kernel, want_ctx=want_ctx),
        out_shape=out_shape,
        grid=(b,),
        in_specs=[seq(l, 2 * LRU_W), seq(c, 2 * LRU_W)] + [full(a) for a in params],
        out_specs=out_specs,
        scratch_shapes=scratch,
        compiler_params=_cparams("parallel"),
        name="rglru_mixer",
    )(p_l, p_c, *params)
    return (res[0], res[1]) if want_ctx else (res[0], None)


def _outproj_kernel(hg_ref, at_ref, lr_ref, x_ref, w_ref, g1_ref, ng_ref, sh_ref, sc_ref, x1_ref, h2_ref):
    ycat = jnp.concatenate([hg_ref[...], at_ref[...], lr_ref[...]], axis=1)
    y = jnp.dot(ycat, w_ref[...], preferred_element_type=F32)
    x1 = x_ref[...] + g1_ref[...] * y
    x1_ref[...] = x1
    h2_ref[...] = _norm_mod(x1, ng_ref[...], sh_ref[...], sc_ref[...]).astype(h2_ref.dtype)


def _outproj(hg, at, lr, x, w_out, g1, ng, shift, scale, *, tm):
    b, l, d = x.shape
    tm = min(tm, l)
    row = lambda w: pl.BlockSpec((None, tm, w), lambda bb, i: (bb, i, 0))
    par = lambda a: pl.BlockSpec((None, 1, d), _bmap(a.shape[0], b))
    return pl.pallas_call(
        _outproj_kernel,
        out_shape=[jax.ShapeDtypeStruct((b, l, d), F32), jax.ShapeDtypeStruct((b, l, d), BF16)],
        grid=(b, l // tm),
        in_specs=[row(HG_W), row(ATT_W), row(LRU_W), row(d),
                  pl.BlockSpec(w_out.shape, lambda bb, i: (0, 0)), par(g1),
                  pl.BlockSpec((1, d), lambda bb, i: (0, 0)), par(shift), par(scale)],
        out_specs=[row(d), row(d)],
        compiler_params=_cparams("parallel", "parallel"),
        name="outproj_residual",
    )(hg, at, lr, x, w_out, g1, ng, shift, scale)


def _ffn_kernel(h_ref, hp_ref, hn_ref, x_ref, wu_ref, cw_ref, cb_ref, wd_ref, g2_ref, fg_ref, o_ref,
                *, final_norm, d_ff):
    i = pl.program_id(1)
    nt = pl.num_programs(1)
    tm = h_ref.shape[0]
    prev = jnp.where(i > 0, hp_ref[...], jnp.zeros_like(hp_ref[...]))
    nxt = jnp.where(i < nt - 1, hn_ref[...], jnp.zeros_like(hn_ref[...]))
    hext = jnp.concatenate([prev, h_ref[...], nxt], axis=0)
    ext = tm + 2 * HALO

    def conv(up, c0):
        cols = slice(c0, c0 + FFN_CHUNK)
        acc = cb_ref[:, cols] + cw_ref[1:2, cols] * up[HALO:HALO + tm]
        acc = acc + cw_ref[0:1, cols] * pltpu.roll(up, 1, 0)[HALO:HALO + tm]
        acc = acc + cw_ref[2:3, cols] * pltpu.roll(up, ext - 1, 0)[HALO:HALO + tm]
        return acc

    acc = jnp.zeros((tm, o_ref.shape[-1]), F32)
    for j in range(d_ff // FFN_CHUNK):
        c0 = j * FFN_CHUNK
        ug = jnp.dot(hext, wu_ref[:, c0:c0 + FFN_CHUNK], preferred_element_type=F32)
        uv = jnp.dot(hext, wu_ref[:, d_ff + c0:d_ff + c0 + FFN_CHUNK], preferred_element_type=F32)
        act = _silu(conv(ug, c0)) * conv(uv, d_ff + c0)
        acc = acc + jnp.dot(act.astype(BF16), wd_ref[c0:c0 + FFN_CHUNK, :], preferred_element_type=F32)
    out = x_ref[...] + g2_ref[...] * acc
    if final_norm:
        out = out * lax.rsqrt(jnp.mean(out * out, axis=-1, keepdims=True) + EPS) * fg_ref[...]
    o_ref[...] = out


def _conv_ffn(h2, x1, w_up, conv_w, conv_b, w_down, g2, final_g, *, final_norm, tm):
    b, l, d = x1.shape
    tm = min(tm, l)
    d_ff = w_down.shape[0]
    hb = tm // HALO
    last = l // HALO - 1
    row = pl.BlockSpec((None, tm, d), lambda bb, i: (bb, i, 0))
    prev = pl.BlockSpec((None, HALO, d), lambda bb, i: (bb, jnp.maximum(i * hb - 1, 0), 0))
    nxt = pl.BlockSpec((None, HALO, d), lambda bb, i: (bb, jnp.minimum((i + 1) * hb, last), 0))
    const = lambda a: pl.BlockSpec(a.shape, lambda bb, i: (0,) * a.ndim, pipeline_mode=pl.Buffered(1))
    par = pl.BlockSpec((None, 1, d), _bmap(g2.shape[0], b))
    return pl.pallas_call(
        functools.partial(_ffn_kernel, final_norm=final_norm, d_ff=d_ff),
        out_shape=jax.ShapeDtypeStruct((b, l, d), F32),
        grid=(b, l // tm),
        in_specs=[row, prev, nxt, row, const(w_up), const(conv_w), const(conv_b), const(w_down), par,
                  pl.BlockSpec((1, d), lambda bb, i: (0, 0))],
        out_specs=row,
        compiler_params=_cparams("parallel", "parallel"),
        name="conv_ffn",
    )(h2, h2, h2, x1, w_up, conv_w, conv_b, w_down, g2, final_g)


def _rope_tables(seq_len):
    n_freq = HEAD_DIM // 4
    inv = ROPE_BASE ** (-jnp.arange(n_freq, dtype=F32) / n_freq)
    pos = jnp.arange(seq_len)
    r = (pos // GRID_W).astype(F32)
    col = (pos % GRID_W).astype(F32)
    ang = jnp.concatenate([r[:, None] * inv, col[:, None] * inv], axis=-1)
    cos, sin = jnp.cos(ang), jnp.sin(ang)
    cos_h = jnp.concatenate([cos, cos], axis=-1)
    sin_h = jnp.concatenate([-sin, sin], axis=-1)
    return jnp.tile(cos_h, (1, ATT_KV_HEADS)), jnp.tile(sin_h, (1, ATT_KV_HEADS))


def _block_diag(w):
    n, d, e = w.shape
    eye = jnp.eye(n, dtype=w.dtype)
    return (eye[:, None, :, None] * w[:, :, None, :]).reshape(n * d, n * e)


def kernel(x, c, ctx, c_ctx, ada_w, ada_b, norm_mix_g, norm_ffn_g, w_in, hg_lb_raw, hg_norm_g, att_sink,
           lru_conv_w, lru_conv_b, lru_w_r, lru_b_r, lru_w_i, lru_b_i, lru_lambda, w_out, ffn_w_up,
           ffn_conv_w, ffn_conv_b, ffn_w_down, final_norm_g):
    b, l, d = x.shape
    n_ctx = ctx.shape[1]
    depth = w_in.shape[0]
    assert l % (GRID_W * 8) == 0 and n_ctx % HG_TILE == 0 and d == c.shape[-1]

    rows = -(-(b + 1) // 8) * 8
    crows = jnp.zeros((rows, d), F32).at[:b].set(c).at[b].set(c_ctx)
    mod = _modulation(crows, ada_w, ada_b)
    lbs = _lower_bounds(hg_lb_raw)

    cos, sin = _rope_tables(l)
    head_of = jnp.arange(HG_W) // HG_DK
    blk = (head_of[:, None] == head_of[None, :]).astype(BF16)
    src = jnp.arange(ATT_KV_W)
    dst = jnp.arange(ATT_KV_HEADS * ATT_SLAB)
    rep = ((src[:, None] // HEAD_DIM == dst[None, :] // ATT_SLAB)
           & (src[:, None] % HEAD_DIM == dst[None, :] % HEAD_DIM)).astype(BF16)

    h_ctx = ctx
    for layer in range(depth):
        want_ctx = layer < depth - 1
        m = mod[layer]
        lat = [m[:b, k * d:(k + 1) * d].reshape(b, 1, d) for k in range(6)]
        cx = [m[b:b + 1, k * d:(k + 1) * d].reshape(1, 1, d) for k in range(6)]
        w_in_l = w_in[layer].astype(BF16)
        ng_mix = norm_mix_g[layer].reshape(1, d)
        ng_ffn = norm_ffn_g[layer].reshape(1, d)

        hg_l, q_l, k_l, v_l, lr_l = _inproj(x, ng_mix, lat[0], lat[1], w_in_l, cos, sin, rope=True, tm=512)
        hg_c, q_c, k_c, v_c, lr_c = _inproj(h_ctx, ng_mix, cx[0], cx[1], w_in_l, cos, sin, rope=False, tm=256)

        y_hg_l, y_hg_c = _hgrn2(hg_l, hg_c, lbs[layer].reshape(1, HG_W), hg_norm_g[layer].reshape(1, HG_W),
                                blk, want_ctx=want_ctx)
        sink = att_sink[layer].reshape(1, ATT_HEADS)
        y_at_l = _attention(sink, q_l, k_l, v_l, k_c, v_c, rep, local=True)
        y_lr_l, y_lr_c = _rglru(lr_l, lr_c, lru_conv_w[layer], lru_conv_b[layer].reshape(1, LRU_W),
                                jnp.stack([_block_diag(lru_w_r[layer, dd]) for dd in range(2)]).astype(BF16),
                                lru_b_r[layer],
                                jnp.stack([_block_diag(lru_w_i[layer, dd]) for dd in range(2)]).astype(BF16),
                                lru_b_i[layer], lru_lambda[layer], want_ctx=want_ctx)

        w_out_l = w_out[layer].astype(BF16)
        w_up_l = ffn_w_up[layer].astype(BF16)
        w_dn_l = ffn_w_down[layer].astype(BF16)
        cb_l = ffn_conv_b[layer].reshape(1, -1)
        fg = final_norm_g.reshape(1, d)
        x1, h2 = _outproj(y_hg_l, y_at_l, y_lr_l, x, w_out_l, lat[2], ng_ffn, lat[3], lat[4], tm=512)
        x = _conv_ffn(h2, x1, w_up_l, ffn_conv_w[layer], cb_l, w_dn_l, lat[5], fg,
                      final_norm=not want_ctx, tm=512)
        if want_ctx:
            y_at_c = _attention(sink, q_c, k_c, v_c, k_c, v_c, rep, local=False)
            c1, hc2 = _outproj(y_hg_c, y_at_c, y_lr_c, h_ctx, w_out_l, cx[2], ng_ffn, cx[3], cx[4], tm=256)
            h_ctx = _conv_ffn(hc2, c1, w_up_l, ffn_conv_w[layer], cb_l, w_dn_l, cx[5], fg,
                              final_norm=False, tm=256)
    return x
```

```python
import functools

import jax
import jax.numpy as jnp
from jax import lax
from jax.experimental import pallas as pl
from jax.experimental.pallas import tpu as pltpu

F32 = jnp.float32
BF16 = jnp.bfloat16
ACT = BF16

EPS = 1e-6
GRID_W = 64
HEAD_DIM = 64
ROPE_BASE = 10000.0

HG_HEADS = 4
HG_DK = 64
HG_W = HG_HEADS * HG_DK
HG_CHUNK = 16
HG_TILE = 128

ATT_HEADS = 8
ATT_KV_HEADS = 2
ATT_GROUP = ATT_HEADS // ATT_KV_HEADS
ATT_W = ATT_HEADS * HEAD_DIM
ATT_KV_W = ATT_KV_HEADS * HEAD_DIM
WINDOW = 128
ATT_BLOCK = 128
ATT_SLAB = ATT_GROUP * HEAD_DIM

LRU_W = 256
LRU_BLOCKS = 4
LRU_C = 8.0
LRU_CONV = 4
LRU_CONV_LEFT = 2
LRU_TILE = 128

FFN_CONV = 3
FFN_CHUNK = 256
HALO = 16

IN_HG = 5 * HG_W
IN_W = IN_HG + ATT_W + 2 * ATT_KV_W + 2 * LRU_W

VMEM_LIMIT = 52 * 1024 * 1024


def _cparams(*sem):
    return pltpu.CompilerParams(dimension_semantics=sem, vmem_limit_bytes=VMEM_LIMIT)


def _silu(x):
    return x * jax.nn.sigmoid(x)


def _mod_kernel(c_ref, w_ref, b_ref, o_ref):
    s = _silu(c_ref[...])
    o_ref[...] = jnp.dot(s, w_ref[...], precision=lax.Precision.HIGHEST,
                         preferred_element_type=F32) + b_ref[...]


def _modulation(crows, ada_w, ada_b):
    depth, d, n = ada_w.shape
    rows = crows.shape[0]
    tn = 1536
    return pl.pallas_call(
        _mod_kernel,
        out_shape=jax.ShapeDtypeStruct((depth, rows, n), F32),
        grid=(depth, n // tn),
        in_specs=[pl.BlockSpec((rows, d), lambda l, j: (0, 0)),
                  pl.BlockSpec((None, d, tn), lambda l, j: (l, 0, j)),
                  pl.BlockSpec((None, 1, tn), lambda l, j: (l, 0, j))],
        out_specs=pl.BlockSpec((None, rows, tn), lambda l, j: (l, 0, j)),
        compiler_params=_cparams("parallel", "parallel"),
        name="modulation",
    )(crows, ada_w, ada_b.reshape(depth, 1, n))


def _lb_kernel(raw_ref, o_ref):
    raw = raw_ref[...]
    e = jnp.exp(raw - jnp.max(raw, axis=0, keepdims=True))
    p = e / jnp.sum(e, axis=0, keepdims=True)
    depth = raw.shape[0]
    acc = jnp.zeros_like(p[0:1])
    for l in range(depth):
        acc = acc + p[l:l + 1]
        o_ref[l:l + 1, :] = acc - p[0:1]


def _lower_bounds(raw):
    return pl.pallas_call(_lb_kernel, out_shape=jax.ShapeDtypeStruct(raw.shape, F32),
                          name="hg_lower_bounds")(raw.astype(F32))


def _norm_mod(x, g, shift, scale):
    y = x * lax.rsqrt(jnp.mean(x * x, axis=-1, keepdims=True) + EPS)
    return (y * g) * (1.0 + scale) + shift


def _rot_half(x):
    w = x.shape[-1]
    lane = lax.broadcasted_iota(jnp.int32, x.shape, 1) & (HEAD_DIM - 1)
    return jnp.where(lane < HEAD_DIM // 2, pltpu.roll(x, w - HEAD_DIM // 2, 1),
                     pltpu.roll(x, HEAD_DIM // 2, 1))


def _inproj_kernel(x_ref, g_ref, sh_ref, sc_ref, w_ref, cos_ref, sin_ref,
                   hg_ref, q_ref, k_ref, v_ref, lru_ref, *, rope):
    h = _norm_mod(x_ref[...], g_ref[...], sh_ref[...], sc_ref[...])
    p = jnp.dot(h.astype(BF16), w_ref[...], preferred_element_type=F32)
    o = IN_HG
    hg_ref[...] = p[:, :o].astype(ACT)
    q = p[:, o:o + ATT_W] * (HEAD_DIM ** -0.5)
    k = p[:, o + ATT_W:o + ATT_W + ATT_KV_W]
    if rope:
        cos, sin = cos_ref[...], sin_ref[...]
        cos_q = jnp.concatenate([cos] * (ATT_W // ATT_KV_W), axis=1)
        sin_q = jnp.concatenate([sin] * (ATT_W // ATT_KV_W), axis=1)
        q = q * cos_q + _rot_half(q) * sin_q
        k = k * cos + _rot_half(k) * sin
    q_ref[...] = q.astype(ACT)
    k_ref[...] = k.astype(ACT)
    o += ATT_W + ATT_KV_W
    v_ref[...] = p[:, o:o + ATT_KV_W].astype(ACT)
    o += ATT_KV_W
    lru_ref[...] = p[:, o:].astype(ACT)


def _bmap(nb_arr, b_total):
    return (lambda b, i: (b, 0, 0)) if nb_arr == b_total else (lambda b, i: (0, 0, 0))


def _inproj(x, g, shift, scale, w_in, cos, sin, *, rope, tm):
    b, l, d = x.shape
    tm = min(tm, l)
    row = lambda w: pl.BlockSpec((None, tm, w), lambda bb, i: (bb, i, 0))
    par = lambda a: pl.BlockSpec((None, 1, d), _bmap(a.shape[0], b))
    tab = pl.BlockSpec((tm, ATT_KV_W), lambda bb, i: (i, 0))
    widths = (IN_HG, ATT_W, ATT_KV_W, ATT_KV_W, 2 * LRU_W)
    return pl.pallas_call(
        functools.partial(_inproj_kernel, rope=rope),
        out_shape=[jax.ShapeDtypeStruct((b, l, w), ACT) for w in widths],
        grid=(b, l // tm),
        in_specs=[row(d), pl.BlockSpec((1, d), lambda bb, i: (0, 0)), par(shift), par(scale),
                  pl.BlockSpec(w_in.shape, lambda bb, i: (0, 0)), tab, tab],
        out_specs=[row(w) for w in widths],
        compiler_params=_cparams("parallel", "parallel"),
        name="inproj_rope" if rope else "inproj_ctx",
    )(x, g, shift, scale, w_in, cos, sin)


def _hg_tile(qr, zr, vr, st, lb, esel_ref, rev, emit):
    t, w = zr.shape
    nc = t // HG_CHUNK
    half = HG_CHUNK // 2
    pair_w = w // 2
    z = zr.astype(F32)
    f = lb + (1.0 - lb) * jax.nn.sigmoid(z)
    lf = jnp.log(f)
    kk = (1.0 - lb) * jax.nn.sigmoid(-z)
    row = lax.broadcasted_iota(jnp.int32, (t, w), 0) & (HG_CHUNK - 1)
    lane_head = lax.broadcasted_iota(jnp.int32, (t, w), 1) // HG_DK
    cum = lf
    sft = 1
    while sft < HG_CHUNK:
        cum = cum + jnp.where(row >= sft, pltpu.roll(cum, sft, 0), 0.0)
        sft *= 2
    cum3 = cum.reshape(nc, HG_CHUNK, w)
    tot3 = cum3[:, HG_CHUNK - 1:HG_CHUNK, :]
    if rev:
        rel_base = (lf - cum).reshape(nc, HG_CHUNK, w)
        e_q = tot3 + rel_base
        e_k = -rel_base
    else:
        rel_base = cum3
        e_q = cum3
        e_k = tot3 - cum3
    k3 = kk.reshape(nc, HG_CHUNK, w)
    k_out = (k3 * jnp.exp(e_k)).astype(BF16).reshape(t, w)
    last = slice(HG_CHUNK - 1, HG_CHUNK)
    dec = (jnp.exp(cum3[:, last, :pair_w]), jnp.exp(cum3[:, last, pair_w:]))
    vb = vr.astype(BF16)
    even = (lane_head & 1) == 0
    zb = jnp.zeros((t, w), BF16)
    k_eo = (jnp.where(even, k_out, zb).reshape(nc, HG_CHUNK, w), jnp.where(even, zb, k_out).reshape(nc, HG_CHUNK, w))
    v_eo = (jnp.where(even, vb, zb).reshape(nc, HG_CHUNK, w), jnp.where(even, zb, vb).reshape(nc, HG_CHUNK, w))

    if emit:
        q3 = _silu(qr.astype(F32)).reshape(nc, HG_CHUNK, w)
        row3 = lax.broadcasted_iota(jnp.int32, (nc, HG_CHUNK, w), 1)
        row_part = lax.broadcasted_iota(jnp.int32, (nc, half, w), 1) + (0 if rev else half)
        q_in = (q3 * jnp.exp(e_q)).astype(BF16)
        acc_all = jnp.zeros((t, 2 * HG_DK), F32)
        acc_half = jnp.zeros((t // 2, 2 * HG_DK), F32)
        part = slice(0, half) if rev else slice(half, HG_CHUNK)
        for s in range(HG_CHUNK):
            whole = (s >= half) if rev else (s < half)
            rows = slice(0, HG_CHUNK) if whole else part
            rel = rel_base[:, rows, :] - rel_base[:, s:s + 1, :]
            rr = row3 if whole else row_part
            valid = (rr <= s) if rev else (rr >= s)
            wgt = q3[:, rows, :] * jnp.where(valid, jnp.exp(rel), 0.0) * k3[:, s:s + 1, :]
            sc = jnp.dot(wgt.reshape(-1, w).astype(BF16), esel_ref[s], preferred_element_type=F32)
            if whole:
                acc_all = acc_all + sc
            else:
                acc_half = acc_half + sc
        a3 = acc_all.reshape(nc, HG_CHUNK, 2 * HG_DK)
        h3 = acc_half.reshape(nc, half, 2 * HG_DK)
        if rev:
            sc3 = jnp.concatenate([a3[:, :half] + h3, a3[:, half:]], axis=1)
        else:
            sc3 = jnp.concatenate([a3[:, :half], a3[:, half:] + h3], axis=1)
        scb = sc3.astype(BF16)
        vb3 = vb.reshape(nc, HG_CHUNK, w)
        lh3 = lane_head.reshape(nc, HG_CHUNK, w)
        zb3 = jnp.zeros((nc, HG_CHUNK, w), BF16)
        v_head = [jnp.where(lh3 == h, vb3, zb3) for h in range(HG_HEADS)]
        pad = jnp.zeros((2 * HG_DK - HG_HEADS * HG_CHUNK, w), BF16)

    tn = (((0,), (0,)), ((), ()))
    nt = (((1,), (1,)), ((), ()))
    kv0, kv1 = [], []
    for c in range(nc):
        v2 = jnp.concatenate([v_eo[0][c], v_eo[1][c]], axis=0)
        k2 = jnp.concatenate([k_eo[0][c], k_eo[1][c]], axis=0)
        kv0.append(lax.dot_general(v2[:, :pair_w], k2[:, :pair_w], tn, preferred_element_type=F32))
        kv1.append(lax.dot_general(v2[:, pair_w:], k2[:, pair_w:], tn, preferred_element_type=F32))
    kv0 = jnp.concatenate(kv0, axis=0)
    kv1 = jnp.concatenate(kv1, axis=0)

    st0, st1 = st
    outs = [None] * nc
    order = range(nc - 1, -1, -1) if rev else range(nc)
    for c in order:
        if emit:
            vmat = jnp.concatenate([vh[c] for vh in v_head] + [pad], axis=0)
            o_c = jnp.dot(scb[c], vmat, preferred_element_type=F32)
            qc = q_in[c]
            o_int = jnp.concatenate(
                [lax.dot_general(qc[:, :pair_w], st0.astype(BF16), nt, preferred_element_type=F32),
                 lax.dot_general(qc[:, pair_w:], st1.astype(BF16), nt, preferred_element_type=F32)], axis=1)
            outs[c] = o_c + o_int
        blk_rows = slice(c * pair_w, (c + 1) * pair_w)
        st0 = st0 * dec[0][c] + kv0[blk_rows]
        st1 = st1 * dec[1][c] + kv1[blk_rows]
    o = jnp.concatenate(outs, axis=0) if emit else None
    return o, (st0, st1)


def _hg_seq(p_ref, o_ref, st_ref, lb, esel_ref, emit):
    n = p_ref.shape[0] // HG_TILE
    if emit:
        o_ref[...] = jnp.zeros(o_ref.shape, F32)

    def body(j, carry):
        for d, rev in ((0, False), (1, True)):
            jt = (n - 1 - j) if rev else j
            r0 = pl.multiple_of(jt * HG_TILE, HG_TILE)
            rows = pl.ds(r0, HG_TILE)
            qr = p_ref[rows, 0:HG_W]
            zr = p_ref[rows, (1 + d) * HG_W:(2 + d) * HG_W]
            vr = p_ref[rows, 3 * HG_W:4 * HG_W]
            o, (st0, st1) = _hg_tile(qr, zr, vr, (st_ref[d, 0], st_ref[d, 1]), lb, esel_ref, rev, emit)
            st_ref[d, 0] = st0
            st_ref[d, 1] = st1
            if emit:
                o_ref[rows, :] += o
        return carry

    lax.fori_loop(0, n, body, 0)


def _hg_readout(p_ref, o_ref, y_ref, ng, blk):
    n = p_ref.shape[0] // HG_TILE

    def body(j, carry):
        rows = pl.ds(pl.multiple_of(j * HG_TILE, HG_TILE), HG_TILE)
        o = o_ref[rows, :]
        sq = o * o
        hi = sq.astype(BF16)
        lo = (sq - hi.astype(F32)).astype(BF16)
        ms = (jnp.dot(hi, blk, preferred_element_type=F32)
              + jnp.dot(lo, blk, preferred_element_type=F32)) * (1.0 / HG_DK)
        g = p_ref[rows, 4 * HG_W:5 * HG_W].astype(F32)
        y_ref[rows, :] = (o * lax.rsqrt(ms + EPS) * ng * _silu(g)).astype(y_ref.dtype)
        return carry

    lax.fori_loop(0, n, body, 0)


def _hg_kernel(*refs, want_ctx):
    if want_ctx:
        pl_ref, pc_ref, lb_ref, ng_ref, blk_ref, esel_ref, yl_ref, yc_ref, ol_ref, oc_ref, st_ref = refs
    else:
        pl_ref, pc_ref, lb_ref, ng_ref, blk_ref, esel_ref, yl_ref, ol_ref, st_ref = refs
        yc_ref = oc_ref = None
    lb = lb_ref[...]
    ng = ng_ref[...]
    blk = blk_ref[...]
    st_ref[...] = jnp.zeros(st_ref.shape, F32)
    _hg_seq(pc_ref, oc_ref, st_ref, lb, esel_ref, want_ctx)
    _hg_seq(pl_ref, ol_ref, st_ref, lb, esel_ref, True)
    _hg_readout(pl_ref, ol_ref, yl_ref, ng, blk)
    if want_ctx:
        _hg_readout(pc_ref, oc_ref, yc_ref, ng, blk)


def _hgrn2(p_l, p_c, lb, ng, blk, esel, *, want_ctx):
    b, l, _ = p_l.shape
    c = p_c.shape[1]
    seq = lambda n, w: pl.BlockSpec((None, n, w), lambda bb: (bb, 0, 0))
    vec = pl.BlockSpec((1, HG_W), lambda bb: (0, 0))
    out_shape = [jax.ShapeDtypeStruct((b, l, HG_W), ACT)]
    out_specs = [seq(l, HG_W)]
    scratch = [pltpu.VMEM((l, HG_W), F32)]
    if want_ctx:
        out_shape.append(jax.ShapeDtypeStruct((b, c, HG_W), ACT))
        out_specs.append(seq(c, HG_W))
        scratch.append(pltpu.VMEM((c, HG_W), F32))
    scratch.append(pltpu.VMEM((2, 2, HG_W // 2, HG_W // 2), F32))
    res = pl.pallas_call(
        functools.partial(_hg_kernel, want_ctx=want_ctx),
        out_shape=out_shape,
        grid=(b,),
        in_specs=[seq(l, IN_HG), seq(c, IN_HG), vec, vec,
                  pl.BlockSpec((HG_W, HG_W), lambda bb: (0, 0)),
                  pl.BlockSpec(esel.shape, lambda bb: (0, 0, 0))],
        out_specs=out_specs,
        scratch_shapes=scratch,
        compiler_params=_cparams("parallel"),
        name="hgrn2_mixer",
    )(p_l, p_c, lb, ng, blk, esel)
    return (res[0], res[1]) if want_ctx else (res[0], None)


def _attn_kernel(*refs, local, n_ctx, seq_len):
    if local:
        (sink_ref, q_ref, k_ref, v_ref, kc_ref, vc_ref, rep_ref, o_ref,
         k4_ref, v4_ref, kc4_ref, vc4_ref) = refs
    else:
        sink_ref, q_ref, kc_ref, vc_ref, rep_ref, o_ref, kc4_ref, vc4_ref = refs
    span = 3 * ATT_BLOCK

    rep = rep_ref[...]

    def expand(src, dst, rows):
        step = min(rows, 512)

        def body(i, carry):
            r = pl.ds(pl.multiple_of(i * step, step), step)
            dst[r, :] = jnp.dot(src[r, :], rep, preferred_element_type=F32).astype(BF16)
            return carry

        lax.fori_loop(0, rows // step, body, 0)

    expand(kc_ref, kc4_ref, n_ctx)
    expand(vc_ref, vc4_ref, n_ctx)
    if local:
        expand(k_ref, k4_ref, seq_len)
        expand(v_ref, v4_ref, seq_len)

    rows = ATT_GROUP * ATT_BLOCK
    nt = (((1,), (1,)), ((), ()))

    def block(n, carry):
        lane_head = lax.broadcasted_iota(jnp.int32, (ATT_BLOCK, ATT_SLAB), 1) // HEAD_DIM
        row_head = lax.broadcasted_iota(jnp.int32, (rows, 1), 0) // ATT_BLOCK
        qrows = pl.ds(pl.multiple_of(n * ATT_BLOCK, ATT_BLOCK), ATT_BLOCK)
        if local:
            start = jnp.clip((n - 1) * ATT_BLOCK, 0, seq_len - span)
            start = pl.multiple_of(start, ATT_BLOCK)
            q_pos = n * ATT_BLOCK + (lax.broadcasted_iota(jnp.int32, (rows, span), 0) & (ATT_BLOCK - 1))
            k_pos = start + lax.broadcasted_iota(jnp.int32, (rows, span), 1)
            valid = jnp.abs(q_pos - k_pos) <= WINDOW
        for h in range(ATT_KV_HEADS):
            cols = slice(h * ATT_SLAB, (h + 1) * ATT_SLAB)
            qs = q_ref[qrows, cols]
            zero = jnp.zeros_like(qs)
            lhs = jnp.concatenate([jnp.where(lane_head == g, qs, zero) for g in range(ATT_GROUP)], axis=0)
            sink = jnp.zeros((rows, 1), F32)
            for g in range(ATT_GROUP):
                sink = jnp.where(row_head == g, sink_ref[0, h * ATT_GROUP + g], sink)
            s_ctx = lax.dot_general(lhs, kc4_ref[:, cols], nt, preferred_element_type=F32)
            m = jnp.maximum(jnp.max(s_ctx, axis=-1, keepdims=True), sink)
            if local:
                s_loc = lax.dot_general(lhs, k4_ref[pl.ds(start, span), cols], nt, preferred_element_type=F32)
                s_loc = jnp.where(valid, s_loc, -jnp.inf)
                m = jnp.maximum(m, jnp.max(s_loc, axis=-1, keepdims=True))
            e_ctx = jnp.exp(s_ctx - m)
            den = jnp.sum(e_ctx, axis=-1, keepdims=True) + jnp.exp(sink - m)
            o = jnp.dot(e_ctx.astype(BF16), vc4_ref[:, cols], preferred_element_type=F32)
            if local:
                e_loc = jnp.exp(s_loc - m)
                den = den + jnp.sum(e_loc, axis=-1, keepdims=True)
                o = o + jnp.dot(e_loc.astype(BF16), v4_ref[pl.ds(start, span), cols],
                                preferred_element_type=F32)
            o = o / den
            y = jnp.zeros((ATT_BLOCK, ATT_SLAB), F32)
            for g in range(ATT_GROUP):
                y = jnp.where(lane_head == g, o[g * ATT_BLOCK:(g + 1) * ATT_BLOCK], y)
            o_ref[qrows, cols] = y.astype(o_ref.dtype)
        return carry

    lax.fori_loop(0, seq_len // ATT_BLOCK, block, 0)


def _attention(sink, q, k, v, kc, vc, rep, *, local):
    b, l, _ = q.shape
    c = kc.shape[1]
    whole = lambda n, w: pl.BlockSpec((None, n, w), lambda bb: (bb, 0, 0))
    smem = pl.BlockSpec(memory_space=pltpu.SMEM)
    repspec = pl.BlockSpec(rep.shape, lambda bb: (0, 0))
    wide = ATT_KV_HEADS * ATT_SLAB
    kv = ATT_KV_W
    if local:
        args = (sink, q, k, v, kc, vc, rep)
        in_specs = [smem, whole(l, ATT_W), whole(l, kv), whole(l, kv), whole(c, kv), whole(c, kv), repspec]
        scratch = [pltpu.VMEM((l, wide), BF16), pltpu.VMEM((l, wide), BF16),
                   pltpu.VMEM((c, wide), BF16), pltpu.VMEM((c, wide), BF16)]
    else:
        args = (sink, q, kc, vc, rep)
        in_specs = [smem, whole(l, ATT_W), whole(c, kv), whole(c, kv), repspec]
        scratch = [pltpu.VMEM((c, wide), BF16), pltpu.VMEM((c, wide), BF16)]
    return pl.pallas_call(
        functools.partial(_attn_kernel, local=local, n_ctx=c, seq_len=l),
        out_shape=jax.ShapeDtypeStruct((b, l, ATT_W), ACT),
        grid=(b,),
        in_specs=in_specs,
        out_specs=whole(l, ATT_W),
        scratch_shapes=scratch,
        compiler_params=_cparams("parallel"),
        name="window_attention" if local else "context_attention",
    )(*args)


def _lru_scan_tile(a, u, carry, rev):
    t, w = a.shape
    row = lax.broadcasted_iota(jnp.int32, (t, w), 0)
    sft = 1
    while sft < t:
        if rev:
            ok = row < t - sft
            a_s = pltpu.roll(a, t - sft, 0)
            u_s = pltpu.roll(u, t - sft, 0)
        else:
            ok = row >= sft
            a_s = pltpu.roll(a, sft, 0)
            u_s = pltpu.roll(u, sft, 0)
        u = u + a * jnp.where(ok, u_s, 0.0)
        a = a * jnp.where(ok, a_s, 1.0)
        sft *= 2
    h = u + a * carry
    new_carry = h[0:1, :] if rev else h[t - 1:t, :]
    return h, new_carry


def _lru_seq(p_ref, y_ref, xpad_ref, hsum_ref, carries, prm, emit):
    n_rows = p_ref.shape[0]
    t = min(LRU_TILE, n_rows)
    n = n_rows // t
    cw, cb, wr, br, wi, bi, sp = prm
    pad = 8
    zeros = jnp.zeros((pad, LRU_W), F32)
    xpad_ref[0:pad, :] = zeros
    xpad_ref[pad + n_rows:2 * pad + n_rows, :] = zeros

    def fill(j, carry):
        r0 = pl.multiple_of(j * t, t)
        xpad_ref[pl.ds(r0 + pad, t), :] = p_ref[pl.ds(r0, t), 0:LRU_W].astype(F32)
        if emit:
            hsum_ref[pl.ds(r0, t), :] = jnp.zeros((t, LRU_W), F32)
        return carry

    lax.fori_loop(0, n, fill, 0)
    win_rows = t + 2 * pad

    def body(j, carry):
        new = []
        for d in range(2):
            rev = d == 1
            jt = (n - 1 - j) if rev else j
            r0 = pl.multiple_of(jt * t, t)
            win = xpad_ref[pl.ds(r0, win_rows), :]
            xc = cb
            for k in range(LRU_CONV):
                shift = (LRU_CONV_LEFT - k) % win_rows
                sh = win if shift == 0 else pltpu.roll(win, shift, 0)
                xc = xc + cw[k:k + 1, :] * sh[pad:pad + t, :]
            xb = xc.astype(BF16)
            r = jax.nn.sigmoid(jnp.dot(xb, wr[d], preferred_element_type=F32) + br[d:d + 1, :])
            i = jax.nn.sigmoid(jnp.dot(xb, wi[d], preferred_element_type=F32) + bi[d:d + 1, :])
            log_a = (-LRU_C) * r * sp[d:d + 1, :]
            a = jnp.exp(log_a)
            u = jnp.sqrt(-jnp.tanh(log_a) * (a * a + 1.0)) * (i * xc)
            h, c_new = _lru_scan_tile(a, u, carry[d], rev)
            new.append(c_new)
            if emit:
                hsum_ref[pl.ds(r0, t), :] += h
        return tuple(new)

    carries = lax.fori_loop(0, n, body, carries)

    if emit:
        def out(j, carry):
            rows = pl.ds(pl.multiple_of(j * t, t), t)
            yv = p_ref[rows, LRU_W:2 * LRU_W].astype(F32)
            y_ref[rows, :] = (jax.nn.gelu(yv) * hsum_ref[rows, :]).astype(y_ref.dtype)
            return carry

        lax.fori_loop(0, n, out, 0)
    return carries


def _lru_kernel(*refs, want_ctx):
    if want_ctx:
        (pl_ref, pc_ref, cw_ref, cb_ref, wr_ref, br_ref, wi_ref, bi_ref, lam_ref,
         yl_ref, yc_ref, xpad_ref, hl_ref, hc_ref) = refs
    else:
        (pl_ref, pc_ref, cw_ref, cb_ref, wr_ref, br_ref, wi_ref, bi_ref, lam_ref,
         yl_ref, xpad_ref, hl_ref) = refs
        yc_ref = hc_ref = None
    sp = jax.nn.softplus(-lam_ref[...])
    prm = (cw_ref[...], cb_ref[...], (wr_ref[0], wr_ref[1]), br_ref[...],
           (wi_ref[0], wi_ref[1]), bi_ref[...], sp)
    zero = jnp.zeros((1, LRU_W), F32)
    carries = _lru_seq(pc_ref, yc_ref, xpad_ref, hc_ref, (zero, zero), prm, want_ctx)
    _lru_seq(pl_ref, yl_ref, xpad_ref, hl_ref, carries, prm, True)


def _rglru(p_l, p_c, cw, cb, wr, br, wi, bi, lam, *, want_ctx):
    b, l, _ = p_l.shape
    c = p_c.shape[1]
    seq = lambda n, w: pl.BlockSpec((None, n, w), lambda bb: (bb, 0, 0))
    full = lambda a: pl.BlockSpec(a.shape, lambda bb: (0,) * a.ndim)
    out_shape = [jax.ShapeDtypeStruct((b, l, LRU_W), ACT)]
    out_specs = [seq(l, LRU_W)]
    scratch = [pltpu.VMEM((l + 16, LRU_W), F32), pltpu.VMEM((l, LRU_W), F32)]
    if want_ctx:
        out_shape.append(jax.ShapeDtypeStruct((b, c, LRU_W), ACT))
        out_specs.append(seq(c, LRU_W))
        scratch.append(pltpu.VMEM((c, LRU_W), F32))
    params = (cw, cb, wr, br, wi, bi, lam)
    res = pl.pallas_call(
        functools.partial(_lru_kernel, want_ctx=want_ctx),
        out_shape=out_shape,
        grid=(b,),
        in_specs=[seq(l, 2 * LRU_W), seq(c, 2 * LRU_W)] + [full(a) for a in params],
        out_specs=out_specs,
        scratch_shapes=scratch,
        compiler_params=_cparams("parallel"),
        name="rglru_mixer",
    )(p_l, p_c, *params)
    return (res[0], res[1]) if want_ctx else (res[0], None)


def _outproj_kernel(hg_ref, at_ref, lr_ref, x_ref, w_ref, g1_ref, ng_ref, sh_ref, sc_ref, x1_ref, h2_ref):
    ycat = jnp.concatenate([hg_ref[...], at_ref[...], lr_ref[...]], axis=1)
    y = jnp.dot(ycat, w_ref[...], preferred_element_type=F32)
    x1 = x_ref[...] + g1_ref[...] * y
    x1_ref[...] = x1
    h2_ref[...] = _norm_mod(x1, ng_ref[...], sh_ref[...], sc_ref[...]).astype(h2_ref.dtype)


def _outproj(hg, at, lr, x, w_out, g1, ng, shift, scale, *, tm):
    b, l, d = x.shape
    tm = min(tm, l)
    row = lambda w: pl.BlockSpec((None, tm, w), lambda bb, i: (bb, i, 0))
    par = lambda a: pl.BlockSpec((None, 1, d), _bmap(a.shape[0], b))
    return pl.pallas_call(
        _outproj_kernel,
        out_shape=[jax.ShapeDtypeStruct((b, l, d), F32), jax.ShapeDtypeStruct((b, l, d), BF16)],
        grid=(b, l // tm),
        in_specs=[row(HG_W), row(ATT_W), row(LRU_W), row(d),
                  pl.BlockSpec(w_out.shape, lambda bb, i: (0, 0)), par(g1),
                  pl.BlockSpec((1, d), lambda bb, i: (0, 0)), par(shift), par(scale)],
        out_specs=[row(d), row(d)],
        compiler_params=_cparams("parallel", "parallel"),
        name="outproj_residual",
    )(hg, at, lr, x, w_out, g1, ng, shift, scale)


def _ffn_kernel(h_ref, hp_ref, hn_ref, x_ref, wu_ref, cw_ref, cb_ref, wd_ref, g2_ref, fg_ref, o_ref,
                *, final_norm, d_ff):
    i = pl.program_id(1)
    nt = pl.num_programs(1)
    tm = h_ref.shape[0]
    prev = jnp.where(i > 0, hp_ref[...], jnp.zeros_like(hp_ref[...]))
    nxt = jnp.where(i < nt - 1, hn_ref[...], jnp.zeros_like(hn_ref[...]))
    hext = jnp.concatenate([prev, h_ref[...], nxt], axis=0)
    ext = tm + 2 * HALO
    cw2 = 2 * FFN_CHUNK
    n_chunks = d_ff // FFN_CHUNK

    def up(j):
        return jnp.dot(hext, wu_ref[:, j * cw2:(j + 1) * cw2], preferred_element_type=F32)

    acc = jnp.zeros((tm, o_ref.shape[-1]), F32)
    u_next = up(0)
    for j in range(n_chunks):
        u = u_next
        if j + 1 < n_chunks:
            u_next = up(j + 1)
        cols = slice(j * cw2, (j + 1) * cw2)
        c = cb_ref[:, cols] + cw_ref[1:2, cols] * u[HALO:HALO + tm]
        c = c + cw_ref[0:1, cols] * pltpu.roll(u, 1, 0)[HALO:HALO + tm]
        c = c + cw_ref[2:3, cols] * pltpu.roll(u, ext - 1, 0)[HALO:HALO + tm]
        act = _silu(c[:, :FFN_CHUNK]) * c[:, FFN_CHUNK:]
        acc = acc + jnp.dot(act.astype(BF16), wd_ref[j * FFN_CHUNK:(j + 1) * FFN_CHUNK, :],
                            preferred_element_type=F32)
    out = x_ref[...] + g2_ref[...] * acc
    if final_norm:
        out = out * lax.rsqrt(jnp.mean(out * out, axis=-1, keepdims=True) + EPS) * fg_ref[...]
    o_ref[...] = out


def _conv_ffn(h2, x1, w_up, conv_w, conv_b, w_down, g2, final_g, *, final_norm, tm):
    b, l, d = x1.shape
    tm = min(tm, l)
    d_ff = w_down.shape[0]
    hb = tm // HALO
    last = l // HALO - 1
    row = pl.BlockSpec((None, tm, d), lambda bb, i: (bb, i, 0))
    prev = pl.BlockSpec((None, HALO, d), lambda bb, i: (bb, jnp.maximum(i * hb - 1, 0), 0))
    nxt = pl.BlockSpec((None, HALO, d), lambda bb, i: (bb, jnp.minimum((i + 1) * hb, last), 0))
    const = lambda a: pl.BlockSpec(a.shape, lambda bb, i: (0,) * a.ndim, pipeline_mode=pl.Buffered(1))
    par = pl.BlockSpec((None, 1, d), _bmap(g2.shape[0], b))
    return pl.pallas_call(
        functools.partial(_ffn_kernel, final_norm=final_norm, d_ff=d_ff),
        out_shape=jax.ShapeDtypeStruct((b, l, d), F32),
        grid=(b, l // tm),
        in_specs=[row, prev, nxt, row, const(w_up), const(conv_w), const(conv_b), const(w_down), par,
                  pl.BlockSpec((1, d), lambda bb, i: (0, 0))],
        out_specs=row,
        compiler_params=_cparams("parallel", "parallel"),
        name="conv_ffn",
    )(h2, h2, h2, x1, w_up, conv_w, conv_b, w_down, g2, final_g)


def _rope_tables(seq_len):
    n_freq = HEAD_DIM // 4
    inv = ROPE_BASE ** (-jnp.arange(n_freq, dtype=F32) / n_freq)
    pos = jnp.arange(seq_len)
    r = (pos // GRID_W).astype(F32)
    col = (pos % GRID_W).astype(F32)
    ang = jnp.concatenate([r[:, None] * inv, col[:, None] * inv], axis=-1)
    cos, sin = jnp.cos(ang), jnp.sin(ang)
    cos_h = jnp.concatenate([cos, cos], axis=-1)
    sin_h = jnp.concatenate([-sin, sin], axis=-1)
    return jnp.tile(cos_h, (1, ATT_KV_HEADS)), jnp.tile(sin_h, (1, ATT_KV_HEADS))


def _interleave_halves(a):
    lead = a.shape[:-1]
    n = a.shape[-1] // (2 * FFN_CHUNK)
    return jnp.swapaxes(a.reshape(*lead, 2, n, FFN_CHUNK), -3, -2).reshape(*lead, -1)


def _block_diag(w):
    n, d, e = w.shape
    eye = jnp.eye(n, dtype=w.dtype)
    return (eye[:, None, :, None] * w[:, :, None, :]).reshape(n * d, n * e)


def kernel(x, c, ctx, c_ctx, ada_w, ada_b, norm_mix_g, norm_ffn_g, w_in, hg_lb_raw, hg_norm_g, att_sink,
           lru_conv_w, lru_conv_b, lru_w_r, lru_b_r, lru_w_i, lru_b_i, lru_lambda, w_out, ffn_w_up,
           ffn_conv_w, ffn_conv_b, ffn_w_down, final_norm_g):
    b, l, d = x.shape
    n_ctx = ctx.shape[1]
    depth = w_in.shape[0]
    assert l % (GRID_W * 8) == 0 and n_ctx % HG_TILE == 0 and d == c.shape[-1]

    rows = -(-(b + 1) // 8) * 8
    crows = jnp.zeros((rows, d), F32).at[:b].set(c).at[b].set(c_ctx)
    mod = _modulation(crows, ada_w, ada_b)
    lbs = _lower_bounds(hg_lb_raw)

    cos, sin = _rope_tables(l)
    head_of = jnp.arange(HG_W) // HG_DK
    blk = (head_of[:, None] == head_of[None, :]).astype(BF16)
    col = head_of[None, :, None] * HG_CHUNK + jnp.arange(HG_CHUNK)[:, None, None]
    esel = (col == jnp.arange(2 * HG_DK)[None, None, :]).astype(BF16)
    src = jnp.arange(ATT_KV_W)
    dst = jnp.arange(ATT_KV_HEADS * ATT_SLAB)
    rep = ((src[:, None] // HEAD_DIM == dst[None, :] // ATT_SLAB)
           & (src[:, None] % HEAD_DIM == dst[None, :] % HEAD_DIM)).astype(BF16)

    h_ctx = ctx
    for layer in range(depth):
        want_ctx = layer < depth - 1
        m = mod[layer]
        lat = [m[:b, k * d:(k + 1) * d].reshape(b, 1, d) for k in range(6)]
        cx = [m[b:b + 1, k * d:(k + 1) * d].reshape(1, 1, d) for k in range(6)]
        w_in_l = w_in[layer].astype(BF16)
        ng_mix = norm_mix_g[layer].reshape(1, d)
        ng_ffn = norm_ffn_g[layer].reshape(1, d)

        hg_l, q_l, k_l, v_l, lr_l = _inproj(x, ng_mix, lat[0], lat[1], w_in_l, cos, sin, rope=True, tm=512)
        hg_c, q_c, k_c, v_c, lr_c = _inproj(h_ctx, ng_mix, cx[0], cx[1], w_in_l, cos, sin, rope=False, tm=256)

        y_hg_l, y_hg_c = _hgrn2(hg_l, hg_c, lbs[layer].reshape(1, HG_W), hg_norm_g[layer].reshape(1, HG_W),
                                blk, esel, want_ctx=want_ctx)
        sink = att_sink[layer].reshape(1, ATT_HEADS)
        y_at_l = _attention(sink, q_l, k_l, v_l, k_c, v_c, rep, local=True)
        y_lr_l, y_lr_c = _rglru(lr_l, lr_c, lru_conv_w[layer], lru_conv_b[layer].reshape(1, LRU_W),
                                jnp.stack([_block_diag(lru_w_r[layer, dd]) for dd in range(2)]).astype(BF16),
                                lru_b_r[layer],
                                jnp.stack([_block_diag(lru_w_i[layer, dd]) for dd in range(2)]).astype(BF16),
                                lru_b_i[layer], lru_lambda[layer], want_ctx=want_ctx)

        w_out_l = w_out[layer].astype(BF16)
        w_up_l = _interleave_halves(ffn_w_up[layer]).astype(BF16)
        cw_l = _interleave_halves(ffn_conv_w[layer])
        w_dn_l = ffn_w_down[layer].astype(BF16)
        cb_l = _interleave_halves(ffn_conv_b[layer].reshape(1, -1))
        fg = final_norm_g.reshape(1, d)
        x1, h2 = _outproj(y_hg_l, y_at_l, y_lr_l, x, w_out_l, lat[2], ng_ffn, lat[3], lat[4], tm=512)
        x = _conv_ffn(h2, x1, w_up_l, cw_l, cb_l, w_dn_l, lat[5], fg,
                      final_norm=not want_ctx, tm=512)
        if want_ctx:
            y_at_c = _attention(sink, q_c, k_c, v_c, k_c, v_c, rep, local=False)
            c1, hc2 = _outproj(y_hg_c, y_at_c, y_lr_c, h_ctx, w_out_l, cx[2], ng_ffn, cx[3], cx[4], tm=256)
            h_ctx = _conv_ffn(hc2, c1, w_up_l, cw_l, cb_l, w_dn_l, cx[5], fg,
                              final_norm=False, tm=256)
    return x
```

```python
import functools

import jax
import jax.numpy as jnp
from jax import lax
from jax.experimental import pallas as pl
from jax.experimental.pallas import tpu as pltpu

F32 = jnp.float32
BF16 = jnp.bfloat16
ACT = BF16

EPS = 1e-6
GRID_W = 64
HEAD_DIM = 64
ROPE_BASE = 10000.0

HG_HEADS = 4
HG_DK = 64
HG_W = HG_HEADS * HG_DK
HG_CHUNK = 16
HG_TILE = 128

ATT_HEADS = 8
ATT_KV_HEADS = 2
ATT_GROUP = ATT_HEADS // ATT_KV_HEADS
ATT_W = ATT_HEADS * HEAD_DIM
ATT_KV_W = ATT_KV_HEADS * HEAD_DIM
WINDOW = 128
ATT_BLOCK = 128
ATT_SLAB = ATT_GROUP * HEAD_DIM

LRU_W = 256
LRU_BLOCKS = 4
LRU_C = 8.0
LRU_CONV = 4
LRU_CONV_LEFT = 2
LRU_TILE = 128

FFN_CONV = 3
FFN_CHUNK = 512
HALO = 8

IN_HG = 5 * HG_W
IN_W = IN_HG + ATT_W + 2 * ATT_KV_W + 2 * LRU_W

VMEM_LIMIT = 52 * 1024 * 1024


def _cparams(*sem):
    return pltpu.CompilerParams(dimension_semantics=sem, vmem_limit_bytes=VMEM_LIMIT)


def _silu(x):
    return x * jax.nn.sigmoid(x)


def _mod_kernel(c_ref, w_ref, b_ref, o_ref):
    s = _silu(c_ref[...])
    o_ref[...] = jnp.dot(s, w_ref[...], precision=lax.Precision.HIGHEST,
                         preferred_element_type=F32) + b_ref[...]


def _modulation(crows, ada_w, ada_b):
    depth, d, n = ada_w.shape
    rows = crows.shape[0]
    tn = 1536
    return pl.pallas_call(
        _mod_kernel,
        out_shape=jax.ShapeDtypeStruct((depth, rows, n), F32),
        grid=(depth, n // tn),
        in_specs=[pl.BlockSpec((rows, d), lambda l, j: (0, 0)),
                  pl.BlockSpec((None, d, tn), lambda l, j: (l, 0, j)),
                  pl.BlockSpec((None, 1, tn), lambda l, j: (l, 0, j))],
        out_specs=pl.BlockSpec((None, rows, tn), lambda l, j: (l, 0, j)),
        compiler_params=_cparams("parallel", "parallel"),
        name="modulation",
    )(crows, ada_w, ada_b.reshape(depth, 1, n))


def _lb_kernel(raw_ref, o_ref):
    raw = raw_ref[...]
    e = jnp.exp(raw - jnp.max(raw, axis=0, keepdims=True))
    p = e / jnp.sum(e, axis=0, keepdims=True)
    depth = raw.shape[0]
    acc = jnp.zeros_like(p[0:1])
    for l in range(depth):
        acc = acc + p[l:l + 1]
        o_ref[l:l + 1, :] = acc - p[0:1]


def _lower_bounds(raw):
    return pl.pallas_call(_lb_kernel, out_shape=jax.ShapeDtypeStruct(raw.shape, F32),
                          name="hg_lower_bounds")(raw.astype(F32))


def _norm_mod(x, g, shift, scale):
    y = x * lax.rsqrt(jnp.mean(x * x, axis=-1, keepdims=True) + EPS)
    return (y * g) * (1.0 + scale) + shift


def _rot_half(x):
    w = x.shape[-1]
    lane = lax.broadcasted_iota(jnp.int32, x.shape, 1) & (HEAD_DIM - 1)
    return jnp.where(lane < HEAD_DIM // 2, pltpu.roll(x, w - HEAD_DIM // 2, 1),
                     pltpu.roll(x, HEAD_DIM // 2, 1))


def _inproj_kernel(x_ref, g_ref, sh_ref, sc_ref, w_ref, cos_ref, sin_ref,
                   hg_ref, q_ref, k_ref, v_ref, lru_ref, *, rope):
    h = _norm_mod(x_ref[...], g_ref[...], sh_ref[...], sc_ref[...])
    p = jnp.dot(h.astype(BF16), w_ref[...], preferred_element_type=F32)
    o = IN_HG
    hg_ref[...] = p[:, :o].astype(ACT)
    q = p[:, o:o + ATT_W] * (HEAD_DIM ** -0.5)
    k = p[:, o + ATT_W:o + ATT_W + ATT_KV_W]
    if rope:
        cos, sin = cos_ref[...], sin_ref[...]
        cos_q = jnp.concatenate([cos] * (ATT_W // ATT_KV_W), axis=1)
        sin_q = jnp.concatenate([sin] * (ATT_W // ATT_KV_W), axis=1)
        q = q * cos_q + _rot_half(q) * sin_q
        k = k * cos + _rot_half(k) * sin
    q_ref[...] = q.astype(ACT)
    k_ref[...] = k.astype(ACT)
    o += ATT_W + ATT_KV_W
    v_ref[...] = p[:, o:o + ATT_KV_W].astype(ACT)
    o += ATT_KV_W
    lru_ref[...] = p[:, o:].astype(ACT)


def _bmap(nb_arr, b_total):
    return (lambda b, i: (b, 0, 0)) if nb_arr == b_total else (lambda b, i: (0, 0, 0))


def _inproj(x, g, shift, scale, w_in, cos, sin, *, rope, tm):
    b, l, d = x.shape
    tm = min(tm, l)
    row = lambda w: pl.BlockSpec((None, tm, w), lambda bb, i: (bb, i, 0))
    par = lambda a: pl.BlockSpec((None, 1, d), _bmap(a.shape[0], b))
    tab = pl.BlockSpec((tm, ATT_KV_W), lambda bb, i: (i, 0))
    widths = (IN_HG, ATT_W, ATT_KV_W, ATT_KV_W, 2 * LRU_W)
    return pl.pallas_call(
        functools.partial(_inproj_kernel, rope=rope),
        out_shape=[jax.ShapeDtypeStruct((b, l, w), ACT) for w in widths],
        grid=(b, l // tm),
        in_specs=[row(d), pl.BlockSpec((1, d), lambda bb, i: (0, 0)), par(shift), par(scale),
                  pl.BlockSpec(w_in.shape, lambda bb, i: (0, 0)), tab, tab],
        out_specs=[row(w) for w in widths],
        compiler_params=_cparams("parallel", "parallel"),
        name="inproj_rope" if rope else "inproj_ctx",
    )(x, g, shift, scale, w_in, cos, sin)


def _hg_tile(qr, zr, vr, st, lb, esel_ref, rev, emit):
    t, w = zr.shape
    nc = t // HG_CHUNK
    half = HG_CHUNK // 2
    pair_w = w // 2
    z = zr.astype(F32)
    f = lb + (1.0 - lb) * jax.nn.sigmoid(z)
    lf = jnp.log(f)
    lk3 = (jnp.log(1.0 - lb) - jax.nn.softplus(z)).reshape(nc, HG_CHUNK, w)
    row = lax.broadcasted_iota(jnp.int32, (t, w), 0) & (HG_CHUNK - 1)
    lane_head = lax.broadcasted_iota(jnp.int32, (t, w), 1) // HG_DK
    cum = lf
    sft = 1
    while sft < HG_CHUNK:
        cum = cum + jnp.where(row >= sft, pltpu.roll(cum, sft, 0), 0.0)
        sft *= 2
    cum3 = cum.reshape(nc, HG_CHUNK, w)
    tot3 = cum3[:, HG_CHUNK - 1:HG_CHUNK, :]
    if rev:
        rel_base = (lf - cum).reshape(nc, HG_CHUNK, w)
        e_q = tot3 + rel_base
        e_k = -rel_base
    else:
        rel_base = cum3
        e_q = cum3
        e_k = tot3 - cum3
    k_out = jnp.exp(lk3 + e_k).astype(BF16).reshape(t, w)
    last = slice(HG_CHUNK - 1, HG_CHUNK)
    dec = (jnp.exp(cum3[:, last, :pair_w]), jnp.exp(cum3[:, last, pair_w:]))
    vb = vr.astype(BF16)
    even = (lane_head & 1) == 0
    zb = jnp.zeros((t, w), BF16)
    k_eo = (jnp.where(even, k_out, zb).reshape(nc, HG_CHUNK, w), jnp.where(even, zb, k_out).reshape(nc, HG_CHUNK, w))
    v_eo = (jnp.where(even, vb, zb).reshape(nc, HG_CHUNK, w), jnp.where(even, zb, vb).reshape(nc, HG_CHUNK, w))

    if emit:
        q3 = _silu(qr.astype(F32)).reshape(nc, HG_CHUNK, w)
        row3 = lax.broadcasted_iota(jnp.int32, (nc, HG_CHUNK, w), 1)
        row_part = lax.broadcasted_iota(jnp.int32, (nc, half, w), 1) + (0 if rev else half)
        q_in = (q3 * jnp.exp(e_q)).astype(BF16)
        rel_key = rel_base - lk3
        acc_all = jnp.zeros((t, 2 * HG_DK), F32)
        acc_half = jnp.zeros((t // 2, 2 * HG_DK), F32)
        part = slice(0, half) if rev else slice(half, HG_CHUNK)
        for s in range(HG_CHUNK):
            whole = (s >= half) if rev else (s < half)
            rows = slice(0, HG_CHUNK) if whole else part
            rel = rel_base[:, rows, :] - rel_key[:, s:s + 1, :]
            rr = row3 if whole else row_part
            valid = (rr <= s) if rev else (rr >= s)
            wgt = q3[:, rows, :] * jnp.where(valid, jnp.exp(rel), 0.0)
            sc = jnp.dot(wgt.reshape(-1, w).astype(BF16), esel_ref[s], preferred_element_type=F32)
            if whole:
                acc_all = acc_all + sc
            else:
                acc_half = acc_half + sc
        a3 = acc_all.reshape(nc, HG_CHUNK, 2 * HG_DK)
        h3 = acc_half.reshape(nc, half, 2 * HG_DK)
        if rev:
            sc3 = jnp.concatenate([a3[:, :half] + h3, a3[:, half:]], axis=1)
        else:
            sc3 = jnp.concatenate([a3[:, :half], a3[:, half:] + h3], axis=1)
        scb = sc3.astype(BF16)
        vb3 = vb.reshape(nc, HG_CHUNK, w)
        lh3 = lane_head.reshape(nc, HG_CHUNK, w)
        zb3 = jnp.zeros((nc, HG_CHUNK, w), BF16)
        v_head = [jnp.where(lh3 == h, vb3, zb3) for h in range(HG_HEADS)]
        pad = jnp.zeros((2 * HG_DK - HG_HEADS * HG_CHUNK, w), BF16)

    tn = (((0,), (0,)), ((), ()))
    nt = (((1,), (1,)), ((), ()))
    kv0, kv1 = [], []
    for c in range(nc):
        v2 = jnp.concatenate([v_eo[0][c], v_eo[1][c]], axis=0)
        k2 = jnp.concatenate([k_eo[0][c], k_eo[1][c]], axis=0)
        kv0.append(lax.dot_general(v2[:, :pair_w], k2[:, :pair_w], tn, preferred_element_type=F32))
        kv1.append(lax.dot_general(v2[:, pair_w:], k2[:, pair_w:], tn, preferred_element_type=F32))
    kv0 = jnp.concatenate(kv0, axis=0)
    kv1 = jnp.concatenate(kv1, axis=0)

    st0, st1 = st
    outs = [None] * nc
    order = range(nc - 1, -1, -1) if rev else range(nc)
    for c in order:
        if emit:
            vmat = jnp.concatenate([vh[c] for vh in v_head] + [pad], axis=0)
            o_c = jnp.dot(scb[c], vmat, preferred_element_type=F32)
            qc = q_in[c]
            o_int = jnp.concatenate(
                [lax.dot_general(qc[:, :pair_w], st0.astype(BF16), nt, preferred_element_type=F32),
                 lax.dot_general(qc[:, pair_w:], st1.astype(BF16), nt, preferred_element_type=F32)], axis=1)
            outs[c] = o_c + o_int
        blk_rows = slice(c * pair_w, (c + 1) * pair_w)
        st0 = st0 * dec[0][c] + kv0[blk_rows]
        st1 = st1 * dec[1][c] + kv1[blk_rows]
    o = jnp.concatenate(outs, axis=0) if emit else None
    return o, (st0, st1)


def _hg_seq(p_ref, o_ref, st_ref, lb, esel_ref, emit):
    n = p_ref.shape[0] // HG_TILE
    if emit:
        o_ref[...] = jnp.zeros(o_ref.shape, F32)

    def body(j, carry):
        for d, rev in ((0, False), (1, True)):
            jt = (n - 1 - j) if rev else j
            r0 = pl.multiple_of(jt * HG_TILE, HG_TILE)
            rows = pl.ds(r0, HG_TILE)
            qr = p_ref[rows, 0:HG_W]
            zr = p_ref[rows, (1 + d) * HG_W:(2 + d) * HG_W]
            vr = p_ref[rows, 3 * HG_W:4 * HG_W]
            o, (st0, st1) = _hg_tile(qr, zr, vr, (st_ref[d, 0], st_ref[d, 1]), lb, esel_ref, rev, emit)
            st_ref[d, 0] = st0
            st_ref[d, 1] = st1
            if emit:
                o_ref[rows, :] += o
        return carry

    lax.fori_loop(0, n, body, 0, unroll=2)


def _hg_readout(p_ref, o_ref, y_ref, ng, blk):
    n = p_ref.shape[0] // HG_TILE

    def body(j, carry):
        rows = pl.ds(pl.multiple_of(j * HG_TILE, HG_TILE), HG_TILE)
        o = o_ref[rows, :]
        sq = o * o
        hi = sq.astype(BF16)
        lo = (sq - hi.astype(F32)).astype(BF16)
        ms = (jnp.dot(hi, blk, preferred_element_type=F32)
              + jnp.dot(lo, blk, preferred_element_type=F32)) * (1.0 / HG_DK)
        g = p_ref[rows, 4 * HG_W:5 * HG_W].astype(F32)
        y_ref[rows, :] = (o * lax.rsqrt(ms + EPS) * ng * _silu(g)).astype(y_ref.dtype)
        return carry

    lax.fori_loop(0, n, body, 0)


def _hg_kernel(*refs, want_ctx):
    if want_ctx:
        pl_ref, pc_ref, lb_ref, ng_ref, blk_ref, esel_ref, yl_ref, yc_ref, ol_ref, oc_ref, st_ref = refs
    else:
        pl_ref, pc_ref, lb_ref, ng_ref, blk_ref, esel_ref, yl_ref, ol_ref, st_ref = refs
        yc_ref = oc_ref = None
    lb = lb_ref[...]
    ng = ng_ref[...]
    blk = blk_ref[...]
    st_ref[...] = jnp.zeros(st_ref.shape, F32)
    _hg_seq(pc_ref, oc_ref, st_ref, lb, esel_ref, want_ctx)
    _hg_seq(pl_ref, ol_ref, st_ref, lb, esel_ref, True)
    _hg_readout(pl_ref, ol_ref, yl_ref, ng, blk)
    if want_ctx:
        _hg_readout(pc_ref, oc_ref, yc_ref, ng, blk)


def _hgrn2(p_l, p_c, lb, ng, blk, esel, *, want_ctx):
    b, l, _ = p_l.shape
    c = p_c.shape[1]
    seq = lambda n, w: pl.BlockSpec((None, n, w), lambda bb: (bb, 0, 0))
    vec = pl.BlockSpec((1, HG_W), lambda bb: (0, 0))
    out_shape = [jax.ShapeDtypeStruct((b, l, HG_W), ACT)]
    out_specs = [seq(l, HG_W)]
    scratch = [pltpu.VMEM((l, HG_W), F32)]
    if want_ctx:
        out_shape.append(jax.ShapeDtypeStruct((b, c, HG_W), ACT))
        out_specs.append(seq(c, HG_W))
        scratch.append(pltpu.VMEM((c, HG_W), F32))
    scratch.append(pltpu.VMEM((2, 2, HG_W // 2, HG_W // 2), F32))
    res = pl.pallas_call(
        functools.partial(_hg_kernel, want_ctx=want_ctx),
        out_shape=out_shape,
        grid=(b,),
        in_specs=[seq(l, IN_HG), seq(c, IN_HG), vec, vec,
                  pl.BlockSpec((HG_W, HG_W), lambda bb: (0, 0)),
                  pl.BlockSpec(esel.shape, lambda bb: (0, 0, 0))],
        out_specs=out_specs,
        scratch_shapes=scratch,
        compiler_params=_cparams("parallel"),
        name="hgrn2_mixer",
    )(p_l, p_c, lb, ng, blk, esel)
    return (res[0], res[1]) if want_ctx else (res[0], None)


def _attn_kernel(*refs, local, n_ctx, seq_len):
    if local:
        (sink_ref, q_ref, k_ref, v_ref, kc_ref, vc_ref, rep_ref, o_ref,
         k4_ref, v4_ref, kc4_ref, vc4_ref) = refs
    else:
        sink_ref, q_ref, kc_ref, vc_ref, rep_ref, o_ref, kc4_ref, vc4_ref = refs
    span = 3 * ATT_BLOCK

    rep = rep_ref[...]

    def expand(src, dst, rows):
        step = min(rows, 512)

        def body(i, carry):
            r = pl.ds(pl.multiple_of(i * step, step), step)
            dst[r, :] = jnp.dot(src[r, :], rep, preferred_element_type=F32).astype(BF16)
            return carry

        lax.fori_loop(0, rows // step, body, 0)

    expand(kc_ref, kc4_ref, n_ctx)
    expand(vc_ref, vc4_ref, n_ctx)
    if local:
        expand(k_ref, k4_ref, seq_len)
        expand(v_ref, v4_ref, seq_len)

    rows = ATT_GROUP * ATT_BLOCK
    nt = (((1,), (1,)), ((), ()))
    nb = seq_len // ATT_BLOCK

    def bias_for(offset):
        r = lax.broadcasted_iota(jnp.int32, (ATT_BLOCK, span), 0)
        c = lax.broadcasted_iota(jnp.int32, (ATT_BLOCK, span), 1)
        b = jnp.where(jnp.abs(offset + r - c) <= WINDOW, 0.0, -jnp.inf).astype(F32)
        return jnp.concatenate([b] * ATT_GROUP, axis=0)

    def block(n, start, bias):
        lane_head = lax.broadcasted_iota(jnp.int32, (ATT_BLOCK, ATT_SLAB), 1) // HEAD_DIM
        row_head = lax.broadcasted_iota(jnp.int32, (rows, 1), 0) // ATT_BLOCK
        qrows = pl.ds(pl.multiple_of(n * ATT_BLOCK, ATT_BLOCK), ATT_BLOCK)
        for h in range(ATT_KV_HEADS):
            cols = slice(h * ATT_SLAB, (h + 1) * ATT_SLAB)
            qs = q_ref[qrows, cols]
            zero = jnp.zeros_like(qs)
            lhs = jnp.concatenate([jnp.where(lane_head == g, qs, zero) for g in range(ATT_GROUP)], axis=0)
            sink = jnp.zeros((rows, 1), F32)
            for g in range(ATT_GROUP):
                sink = jnp.where(row_head == g, sink_ref[0, h * ATT_GROUP + g], sink)
            s_ctx = lax.dot_general(lhs, kc4_ref[:, cols], nt, preferred_element_type=F32)
            m = jnp.maximum(jnp.max(s_ctx, axis=-1, keepdims=True), sink)
            if local:
                s_loc = lax.dot_general(lhs, k4_ref[pl.ds(start, span), cols], nt, preferred_element_type=F32)
                s_loc = s_loc + bias
                m = jnp.maximum(m, jnp.max(s_loc, axis=-1, keepdims=True))
            e_ctx = jnp.exp(s_ctx - m)
            den = jnp.sum(e_ctx, axis=-1, keepdims=True) + jnp.exp(sink - m)
            o = jnp.dot(e_ctx.astype(BF16), vc4_ref[:, cols], preferred_element_type=F32)
            if local:
                e_loc = jnp.exp(s_loc - m)
                den = den + jnp.sum(e_loc, axis=-1, keepdims=True)
                o = o + jnp.dot(e_loc.astype(BF16), v4_ref[pl.ds(start, span), cols],
                                preferred_element_type=F32)
            o = o / den
            y = jnp.zeros((ATT_BLOCK, ATT_SLAB), F32)
            for g in range(ATT_GROUP):
                y = jnp.where(lane_head == g, o[g * ATT_BLOCK:(g + 1) * ATT_BLOCK], y)
            o_ref[qrows, cols] = y.astype(o_ref.dtype)

    if not local:
        def ctx_block(n, carry):
            block(n, None, None)
            return carry

        lax.fori_loop(0, nb, ctx_block, 0)
        return

    block(0, 0, bias_for(0))
    bias_mid = bias_for(ATT_BLOCK)

    def mid_block(n, carry):
        block(n, pl.multiple_of((n - 1) * ATT_BLOCK, ATT_BLOCK), bias_mid)
        return carry

    lax.fori_loop(1, nb - 1, mid_block, 0, unroll=3)
    block(nb - 1, seq_len - span, bias_for(2 * ATT_BLOCK))


def _attention(sink, q, k, v, kc, vc, rep, *, local):
    b, l, _ = q.shape
    c = kc.shape[1]
    whole = lambda n, w: pl.BlockSpec((None, n, w), lambda bb: (bb, 0, 0))
    smem = pl.BlockSpec(memory_space=pltpu.SMEM)
    repspec = pl.BlockSpec(rep.shape, lambda bb: (0, 0))
    wide = ATT_KV_HEADS * ATT_SLAB
    kv = ATT_KV_W
    if local:
        args = (sink, q, k, v, kc, vc, rep)
        in_specs = [smem, whole(l, ATT_W), whole(l, kv), whole(l, kv), whole(c, kv), whole(c, kv), repspec]
        scratch = [pltpu.VMEM((l, wide), BF16), pltpu.VMEM((l, wide), BF16),
                   pltpu.VMEM((c, wide), BF16), pltpu.VMEM((c, wide), BF16)]
    else:
        args = (sink, q, kc, vc, rep)
        in_specs = [smem, whole(l, ATT_W), whole(c, kv), whole(c, kv), repspec]
        scratch = [pltpu.VMEM((c, wide), BF16), pltpu.VMEM((c, wide), BF16)]
    return pl.pallas_call(
        functools.partial(_attn_kernel, local=local, n_ctx=c, seq_len=l),
        out_shape=jax.ShapeDtypeStruct((b, l, ATT_W), ACT),
        grid=(b,),
        in_specs=in_specs,
        out_specs=whole(l, ATT_W),
        scratch_shapes=scratch,
        compiler_params=_cparams("parallel"),
        name="window_attention" if local else "context_attention",
    )(*args)


def _lru_scan_tile(a, u, carry, rev):
    t, w = a.shape
    row = lax.broadcasted_iota(jnp.int32, (t, w), 0)
    sft = 1
    while sft < t:
        if sft < 8:
            ok = (row < t - sft) if rev else (row >= sft)
            shift = (t - sft) if rev else sft
            u_p = jnp.where(ok, pltpu.roll(u, shift, 0), 0.0)
            a_p = jnp.where(ok, pltpu.roll(a, shift, 0), 1.0)
        elif rev:
            u_p = jnp.concatenate([u[sft:], jnp.zeros((sft, w), F32)], axis=0)
            a_p = jnp.concatenate([a[sft:], jnp.ones((sft, w), F32)], axis=0)
        else:
            u_p = jnp.concatenate([jnp.zeros((sft, w), F32), u[:t - sft]], axis=0)
            a_p = jnp.concatenate([jnp.ones((sft, w), F32), a[:t - sft]], axis=0)
        u = u + a * u_p
        a = a * a_p
        sft *= 2
    h = u + a * carry
    new_carry = h[0:1, :] if rev else h[t - 1:t, :]
    return h, new_carry


def _lru_seq(p_ref, y_ref, xpad_ref, hsum_ref, carries, prm, emit):
    n_rows = p_ref.shape[0]
    t = min(LRU_TILE, n_rows)
    n = n_rows // t
    cw, cb, wr, br, wi, bi, sp = prm
    pad = 8
    zeros = jnp.zeros((pad, LRU_W), F32)
    xpad_ref[0:pad, :] = zeros
    xpad_ref[pad + n_rows:2 * pad + n_rows, :] = zeros

    def fill(j, carry):
        r0 = pl.multiple_of(j * t, t)
        xpad_ref[pl.ds(r0 + pad, t), :] = p_ref[pl.ds(r0, t), 0:LRU_W].astype(F32)
        if emit:
            hsum_ref[pl.ds(r0, t), :] = jnp.zeros((t, LRU_W), F32)
        return carry

    lax.fori_loop(0, n, fill, 0)
    win_rows = t + 2 * pad

    def body(j, carry):
        new = []
        for d in range(2):
            rev = d == 1
            jt = (n - 1 - j) if rev else j
            r0 = pl.multiple_of(jt * t, t)
            win = xpad_ref[pl.ds(r0, win_rows), :]
            xc = cb
            for k in range(LRU_CONV):
                shift = (LRU_CONV_LEFT - k) % win_rows
                sh = win if shift == 0 else pltpu.roll(win, shift, 0)
                xc = xc + cw[k:k + 1, :] * sh[pad:pad + t, :]
            xb = xc.astype(BF16)
            r = jax.nn.sigmoid(jnp.dot(xb, wr[d], preferred_element_type=F32) + br[d:d + 1, :])
            i = jax.nn.sigmoid(jnp.dot(xb, wi[d], preferred_element_type=F32) + bi[d:d + 1, :])
            log_a = (-LRU_C) * r * sp[d:d + 1, :]
            a = jnp.exp(log_a)
            u = jnp.sqrt(-jnp.tanh(log_a) * (a * a + 1.0)) * (i * xc)
            h, c_new = _lru_scan_tile(a, u, carry[d], rev)
            new.append(c_new)
            if emit:
                hsum_ref[pl.ds(r0, t), :] += h
        return tuple(new)

    carries = lax.fori_loop(0, n, body, carries)

    if emit:
        def out(j, carry):
            rows = pl.ds(pl.multiple_of(j * t, t), t)
            yv = p_ref[rows, LRU_W:2 * LRU_W].astype(F32)
            y_ref[rows, :] = (jax.nn.gelu(yv) * hsum_ref[rows, :]).astype(y_ref.dtype)
            return carry

        lax.fori_loop(0, n, out, 0)
    return carries


def _lru_kernel(*refs, want_ctx):
    if want_ctx:
        (pl_ref, pc_ref, cw_ref, cb_ref, wr_ref, br_ref, wi_ref, bi_ref, lam_ref,
         yl_ref, yc_ref, xpad_ref, hl_ref, hc_ref) = refs
    else:
        (pl_ref, pc_ref, cw_ref, cb_ref, wr_ref, br_ref, wi_ref, bi_ref, lam_ref,
         yl_ref, xpad_ref, hl_ref) = refs
        yc_ref = hc_ref = None
    sp = jax.nn.softplus(-lam_ref[...])
    prm = (cw_ref[...], cb_ref[...], (wr_ref[0], wr_ref[1]), br_ref[...],
           (wi_ref[0], wi_ref[1]), bi_ref[...], sp)
    zero = jnp.zeros((1, LRU_W), F32)
    carries = _lru_seq(pc_ref, yc_ref, xpad_ref, hc_ref, (zero, zero), prm, want_ctx)
    _lru_seq(pl_ref, yl_ref, xpad_ref, hl_ref, carries, prm, True)


def _rglru(p_l, p_c, cw, cb, wr, br, wi, bi, lam, *, want_ctx):
    b, l, _ = p_l.shape
    c = p_c.shape[1]
    seq = lambda n, w: pl.BlockSpec((None, n, w), lambda bb: (bb, 0, 0))
    full = lambda a: pl.BlockSpec(a.shape, lambda bb: (0,) * a.ndim)
    out_shape = [jax.ShapeDtypeStruct((b, l, LRU_W), ACT)]
    out_specs = [seq(l, LRU_W)]
    scratch = [pltpu.VMEM((l + 16, LRU_W), F32), pltpu.VMEM((l, LRU_W), F32)]
    if want_ctx:
        out_shape.append(jax.ShapeDtypeStruct((b, c, LRU_W), ACT))
        out_specs.append(seq(c, LRU_W))
        scratch.append(pltpu.VMEM((c, LRU_W), F32))
    params = (cw, cb, wr, br, wi, bi, lam)
    res = pl.pallas_call(
        functools.partial(_lru_kernel, want_ctx=want_ctx),
        out_shape=out_shape,
        grid=(b,),
        in_specs=[seq(l, 2 * LRU_W), seq(c, 2 * LRU_W)] + [full(a) for a in params],
        out_specs=out_specs,
        scratch_shapes=scratch,
        compiler_params=_cparams("parallel"),
        name="rglru_mixer",
    )(p_l, p_c, *params)
    return (res[0], res[1]) if want_ctx else (res[0], None)


def _outproj_kernel(hg_ref, at_ref, lr_ref, x_ref, w_ref, g1_ref, x1_ref):
    ycat = jnp.concatenate([hg_ref[...], at_ref[...], lr_ref[...]], axis=1)
    y = jnp.dot(ycat, w_ref[...], preferred_element_type=F32)
    x1_ref[...] = x_ref[...] + g1_ref[...] * y


def _outproj(hg, at, lr, x, w_out, g1, *, tm):
    b, l, d = x.shape
    tm = min(tm, l)
    row = lambda w: pl.BlockSpec((None, tm, w), lambda bb, i: (bb, i, 0))
    return pl.pallas_call(
        _outproj_kernel,
        out_shape=jax.ShapeDtypeStruct((b, l, d), F32),
        grid=(b, l // tm),
        in_specs=[row(HG_W), row(ATT_W), row(LRU_W), row(d),
                  pl.BlockSpec(w_out.shape, lambda bb, i: (0, 0)),
                  pl.BlockSpec((None, 1, d), _bmap(g1.shape[0], b))],
        out_specs=row(d),
        compiler_params=_cparams("parallel", "parallel"),
        name="outproj_residual",
    )(hg, at, lr, x, w_out, g1)


def _ffn_kernel(x_ref, xp_ref, xn_ref, ng_ref, sh_ref, sc_ref, wu_ref, cw_ref, cb_ref, wd_ref, g2_ref, fg_ref,
                o_ref, *, final_norm, d_ff):
    i = pl.program_id(1)
    nt = pl.num_programs(1)
    tm = x_ref.shape[0]
    ext = tm + 2 * HALO
    ng, sh, sc = ng_ref[...], sh_ref[...], sc_ref[...]
    h_prev = jnp.where(i > 0, _norm_mod(xp_ref[...], ng, sh, sc), 0.0)
    h_next = jnp.where(i < nt - 1, _norm_mod(xn_ref[...], ng, sh, sc), 0.0)
    hext = jnp.concatenate([h_prev, _norm_mod(x_ref[...], ng, sh, sc), h_next], axis=0).astype(BF16)
    chunks = [(c0, min(FFN_CHUNK, d_ff - c0)) for c0 in range(0, d_ff, FFN_CHUNK)]

    def up(c0, cw):
        return (jnp.dot(hext, wu_ref[:, c0:c0 + cw], preferred_element_type=F32),
                jnp.dot(hext, wu_ref[:, d_ff + c0:d_ff + c0 + cw], preferred_element_type=F32))

    def conv(u, cols):
        c = cb_ref[:, cols] + cw_ref[1:2, cols] * u[HALO:HALO + tm]
        c = c + cw_ref[0:1, cols] * pltpu.roll(u, 1, 0)[HALO:HALO + tm]
        return c + cw_ref[2:3, cols] * pltpu.roll(u, ext - 1, 0)[HALO:HALO + tm]

    acc = jnp.zeros((tm, o_ref.shape[-1]), F32)
    u_next = up(*chunks[0])
    for j, (c0, cw) in enumerate(chunks):
        ug, uv = u_next
        if j + 1 < len(chunks):
            u_next = up(*chunks[j + 1])
        act = _silu(conv(ug, slice(c0, c0 + cw))) * conv(uv, slice(d_ff + c0, d_ff + c0 + cw))
        acc = acc + jnp.dot(act.astype(BF16), wd_ref[c0:c0 + cw, :], preferred_element_type=F32)
    out = x_ref[...] + g2_ref[...] * acc
    if final_norm:
        out = out * lax.rsqrt(jnp.mean(out * out, axis=-1, keepdims=True) + EPS) * fg_ref[...]
    o_ref[...] = out


def _conv_ffn(x1, ng, shift, scale, w_up, conv_w, conv_b, w_down, g2, final_g, *, final_norm, tm):
    b, l, d = x1.shape
    tm = min(tm, l)
    d_ff = w_down.shape[0]
    hb = tm // HALO
    last = l // HALO - 1
    row = pl.BlockSpec((None, tm, d), lambda bb, i: (bb, i, 0))
    prev = pl.BlockSpec((None, HALO, d), lambda bb, i: (bb, jnp.maximum(i * hb - 1, 0), 0))
    nxt = pl.BlockSpec((None, HALO, d), lambda bb, i: (bb, jnp.minimum((i + 1) * hb, last), 0))
    const = lambda a: pl.BlockSpec(a.shape, lambda bb, i: (0,) * a.ndim, pipeline_mode=pl.Buffered(1))
    par = lambda a: pl.BlockSpec((None, 1, d), _bmap(a.shape[0], b))
    vec = pl.BlockSpec((1, d), lambda bb, i: (0, 0))
    return pl.pallas_call(
        functools.partial(_ffn_kernel, final_norm=final_norm, d_ff=d_ff),
        out_shape=jax.ShapeDtypeStruct((b, l, d), F32),
        grid=(b, l // tm),
        in_specs=[row, prev, nxt, vec, par(shift), par(scale), const(w_up), const(conv_w), const(conv_b),
                  const(w_down), par(g2), vec],
        out_specs=row,
        compiler_params=_cparams("parallel", "parallel"),
        name="conv_ffn",
    )(x1, x1, x1, ng, shift, scale, w_up, conv_w, conv_b, w_down, g2, final_g)


def _rope_tables(seq_len):
    n_freq = HEAD_DIM // 4
    inv = ROPE_BASE ** (-jnp.arange(n_freq, dtype=F32) / n_freq)
    pos = jnp.arange(seq_len)
    r = (pos // GRID_W).astype(F32)
    col = (pos % GRID_W).astype(F32)
    ang = jnp.concatenate([r[:, None] * inv, col[:, None] * inv], axis=-1)
    cos, sin = jnp.cos(ang), jnp.sin(ang)
    cos_h = jnp.concatenate([cos, cos], axis=-1)
    sin_h = jnp.concatenate([-sin, sin], axis=-1)
    return jnp.tile(cos_h, (1, ATT_KV_HEADS)), jnp.tile(sin_h, (1, ATT_KV_HEADS))


def _block_diag(w):
    n, d, e = w.shape
    eye = jnp.eye(n, dtype=w.dtype)
    return (eye[:, None, :, None] * w[:, :, None, :]).reshape(n * d, n * e)


def kernel(x, c, ctx, c_ctx, ada_w, ada_b, norm_mix_g, norm_ffn_g, w_in, hg_lb_raw, hg_norm_g, att_sink,
           lru_conv_w, lru_conv_b, lru_w_r, lru_b_r, lru_w_i, lru_b_i, lru_lambda, w_out, ffn_w_up,
           ffn_conv_w, ffn_conv_b, ffn_w_down, final_norm_g):
    b, l, d = x.shape
    n_ctx = ctx.shape[1]
    depth = w_in.shape[0]
    assert l % (GRID_W * 8) == 0 and n_ctx % HG_TILE == 0 and d == c.shape[-1]

    rows = -(-(b + 1) // 8) * 8
    crows = jnp.zeros((rows, d), F32).at[:b].set(c).at[b].set(c_ctx)
    mod = _modulation(crows, ada_w, ada_b)
    lbs = _lower_bounds(hg_lb_raw)

    cos, sin = _rope_tables(l)
    head_of = jnp.arange(HG_W) // HG_DK
    blk = (head_of[:, None] == head_of[None, :]).astype(BF16)
    col = head_of[None, :, None] * HG_CHUNK + jnp.arange(HG_CHUNK)[:, None, None]
    esel = (col == jnp.arange(2 * HG_DK)[None, None, :]).astype(BF16)
    src = jnp.arange(ATT_KV_W)
    dst = jnp.arange(ATT_KV_HEADS * ATT_SLAB)
    rep = ((src[:, None] // HEAD_DIM == dst[None, :] // ATT_SLAB)
           & (src[:, None] % HEAD_DIM == dst[None, :] % HEAD_DIM)).astype(BF16)

    h_ctx = ctx
    for layer in range(depth):
        want_ctx = layer < depth - 1
        m = mod[layer]
        lat = [m[:b, k * d:(k + 1) * d].reshape(b, 1, d) for k in range(6)]
        cx = [m[b:b + 1, k * d:(k + 1) * d].reshape(1, 1, d) for k in range(6)]
        w_in_l = w_in[layer].astype(BF16)
        ng_mix = norm_mix_g[layer].reshape(1, d)
        ng_ffn = norm_ffn_g[layer].reshape(1, d)

        hg_l, q_l, k_l, v_l, lr_l = _inproj(x, ng_mix, lat[0], lat[1], w_in_l, cos, sin, rope=True, tm=512)
        hg_c, q_c, k_c, v_c, lr_c = _inproj(h_ctx, ng_mix, cx[0], cx[1], w_in_l, cos, sin, rope=False, tm=256)

        y_hg_l, y_hg_c = _hgrn2(hg_l, hg_c, lbs[layer].reshape(1, HG_W), hg_norm_g[layer].reshape(1, HG_W),
                                blk, esel, want_ctx=want_ctx)
        sink = att_sink[layer].reshape(1, ATT_HEADS)
        y_at_l = _attention(sink, q_l, k_l, v_l, k_c, v_c, rep, local=True)
        y_lr_l, y_lr_c = _rglru(lr_l, lr_c, lru_conv_w[layer], lru_conv_b[layer].reshape(1, LRU_W),
                                jnp.stack([_block_diag(lru_w_r[layer, dd]) for dd in range(2)]).astype(BF16),
                                lru_b_r[layer],
                                jnp.stack([_block_diag(lru_w_i[layer, dd]) for dd in range(2)]).astype(BF16),
                                lru_b_i[layer], lru_lambda[layer], want_ctx=want_ctx)

        w_out_l = w_out[layer].astype(BF16)
        w_up_l = ffn_w_up[layer].astype(BF16)
        cw_l = ffn_conv_w[layer]
        w_dn_l = ffn_w_down[layer].astype(BF16)
        cb_l = ffn_conv_b[layer].reshape(1, -1)
        fg = final_norm_g.reshape(1, d)
        x1 = _outproj(y_hg_l, y_at_l, y_lr_l, x, w_out_l, lat[2], tm=512)
        x = _conv_ffn(x1, ng_ffn, lat[3], lat[4], w_up_l, cw_l, cb_l, w_dn_l, lat[5], fg,
                      final_norm=not want_ctx, tm=512)
        if want_ctx:
            y_at_c = _attention(sink, q_c, k_c, v_c, k_c, v_c, rep, local=False)
            c1 = _outproj(y_hg_c, y_at_c, y_lr_c, h_ctx, w_out_l, cx[2], tm=256)
            h_ctx = _conv_ffn(c1, ng_ffn, cx[3], cx[4], w_up_l, cw_l, cb_l, w_dn_l, cx[5], fg,
                              final_norm=False, tm=256)
    return x
```

```python
import functools

import jax
import jax.numpy as jnp
from jax import lax
from jax.experimental import pallas as pl
from jax.experimental.pallas import tpu as pltpu

F32 = jnp.float32
BF16 = jnp.bfloat16
ACT = BF16

EPS = 1e-6
GRID_W = 64
HEAD_DIM = 64
ROPE_BASE = 10000.0

HG_HEADS = 4
HG_DK = 64
HG_W = HG_HEADS * HG_DK
HG_CHUNK = 16
HG_TILE = 128

ATT_HEADS = 8
ATT_KV_HEADS = 2
ATT_GROUP = ATT_HEADS // ATT_KV_HEADS
ATT_W = ATT_HEADS * HEAD_DIM
ATT_KV_W = ATT_KV_HEADS * HEAD_DIM
WINDOW = 128
ATT_BLOCK = 128
ATT_SLAB = ATT_GROUP * HEAD_DIM

LRU_W = 256
LRU_BLOCKS = 4
LRU_C = 8.0
LRU_CONV = 4
LRU_CONV_LEFT = 2
LRU_TILE = 128

FFN_CONV = 3
FFN_CHUNK = 512
HALO = 8

IN_HG = 5 * HG_W
IN_W = IN_HG + ATT_W + 2 * ATT_KV_W + 2 * LRU_W

VMEM_LIMIT = 52 * 1024 * 1024


def _cparams(*sem):
    return pltpu.CompilerParams(dimension_semantics=sem, vmem_limit_bytes=VMEM_LIMIT)


def _silu(x):
    return x * jax.nn.sigmoid(x)


def _mod_kernel(c_ref, w_ref, b_ref, o_ref):
    s = _silu(c_ref[...])
    o_ref[...] = jnp.dot(s, w_ref[...], precision=lax.Precision.HIGHEST,
                         preferred_element_type=F32) + b_ref[...]


def _modulation(crows, ada_w, ada_b):
    depth, d, n = ada_w.shape
    rows = crows.shape[0]
    tn = 1536
    return pl.pallas_call(
        _mod_kernel,
        out_shape=jax.ShapeDtypeStruct((depth, rows, n), F32),
        grid=(depth, n // tn),
        in_specs=[pl.BlockSpec((rows, d), lambda l, j: (0, 0)),
                  pl.BlockSpec((None, d, tn), lambda l, j: (l, 0, j)),
                  pl.BlockSpec((None, 1, tn), lambda l, j: (l, 0, j))],
        out_specs=pl.BlockSpec((None, rows, tn), lambda l, j: (l, 0, j)),
        compiler_params=_cparams("parallel", "parallel"),
        name="modulation",
    )(crows, ada_w, ada_b.reshape(depth, 1, n))


def _lb_kernel(raw_ref, o_ref):
    raw = raw_ref[...]
    e = jnp.exp(raw - jnp.max(raw, axis=0, keepdims=True))
    p = e / jnp.sum(e, axis=0, keepdims=True)
    depth = raw.shape[0]
    acc = jnp.zeros_like(p[0:1])
    for l in range(depth):
        acc = acc + p[l:l + 1]
        o_ref[l:l + 1, :] = acc - p[0:1]


def _lower_bounds(raw):
    return pl.pallas_call(_lb_kernel, out_shape=jax.ShapeDtypeStruct(raw.shape, F32),
                          name="hg_lower_bounds")(raw.astype(F32))


def _norm_mod(x, g, shift, scale):
    y = x * lax.rsqrt(jnp.mean(x * x, axis=-1, keepdims=True) + EPS)
    return (y * g) * (1.0 + scale) + shift


def _rot_half(x):
    w = x.shape[-1]
    lane = lax.broadcasted_iota(jnp.int32, x.shape, 1) & (HEAD_DIM - 1)
    return jnp.where(lane < HEAD_DIM // 2, pltpu.roll(x, w - HEAD_DIM // 2, 1),
                     pltpu.roll(x, HEAD_DIM // 2, 1))


def _inproj_kernel(x_ref, g_ref, sh_ref, sc_ref, w_ref, cos_ref, sin_ref,
                   hg_ref, q_ref, k_ref, v_ref, lru_ref, *, rope):
    h = _norm_mod(x_ref[...], g_ref[...], sh_ref[...], sc_ref[...])
    p = jnp.dot(h.astype(BF16), w_ref[...], preferred_element_type=F32)
    o = IN_HG
    hg_ref[...] = p[:, :o].astype(ACT)
    q = p[:, o:o + ATT_W] * (HEAD_DIM ** -0.5)
    k = p[:, o + ATT_W:o + ATT_W + ATT_KV_W]
    if rope:
        cos, sin = cos_ref[...], sin_ref[...]
        cos_q = jnp.concatenate([cos] * (ATT_W // ATT_KV_W), axis=1)
        sin_q = jnp.concatenate([sin] * (ATT_W // ATT_KV_W), axis=1)
        q = q * cos_q + _rot_half(q) * sin_q
        k = k * cos + _rot_half(k) * sin
    q_ref[...] = q.astype(ACT)
    k_ref[...] = k.astype(ACT)
    o += ATT_W + ATT_KV_W
    v_ref[...] = p[:, o:o + ATT_KV_W].astype(ACT)
    o += ATT_KV_W
    lru_ref[...] = p[:, o:].astype(ACT)


def _bmap(nb_arr, b_total):
    return (lambda b, i: (b, 0, 0)) if nb_arr == b_total else (lambda b, i: (0, 0, 0))


def _inproj(x, g, shift, scale, w_in, cos, sin, *, rope, tm):
    b, l, d = x.shape
    tm = min(tm, l)
    row = lambda w: pl.BlockSpec((None, tm, w), lambda bb, i: (bb, i, 0))
    par = lambda a: pl.BlockSpec((None, 1, d), _bmap(a.shape[0], b))
    tab = pl.BlockSpec((tm, ATT_KV_W), lambda bb, i: (i, 0))
    widths = (IN_HG, ATT_W, ATT_KV_W, ATT_KV_W, 2 * LRU_W)
    return pl.pallas_call(
        functools.partial(_inproj_kernel, rope=rope),
        out_shape=[jax.ShapeDtypeStruct((b, l, w), ACT) for w in widths],
        grid=(b, l // tm),
        in_specs=[row(d), pl.BlockSpec((1, d), lambda bb, i: (0, 0)), par(shift), par(scale),
                  pl.BlockSpec(w_in.shape, lambda bb, i: (0, 0)), tab, tab],
        out_specs=[row(w) for w in widths],
        compiler_params=_cparams("parallel", "parallel"),
        name="inproj_rope" if rope else "inproj_ctx",
    )(x, g, shift, scale, w_in, cos, sin)


def _hg_tile(qr, zr, vr, st, lb, esel_ref, rev, emit):
    t, w = zr.shape
    nc = t // HG_CHUNK
    half = HG_CHUNK // 2
    pair_w = w // 2
    z = zr.astype(F32)
    f = lb + (1.0 - lb) * jax.nn.sigmoid(z)
    lf = jnp.log(f)
    lk3 = (jnp.log(1.0 - lb) - jax.nn.softplus(z)).reshape(nc, HG_CHUNK, w)
    row = lax.broadcasted_iota(jnp.int32, (t, w), 0) & (HG_CHUNK - 1)
    lane_head = lax.broadcasted_iota(jnp.int32, (t, w), 1) // HG_DK
    cum = lf
    sft = 1
    while sft < HG_CHUNK:
        cum = cum + jnp.where(row >= sft, pltpu.roll(cum, sft, 0), 0.0)
        sft *= 2
    cum3 = cum.reshape(nc, HG_CHUNK, w)
    tot3 = cum3[:, HG_CHUNK - 1:HG_CHUNK, :]
    if rev:
        rel_base = (lf - cum).reshape(nc, HG_CHUNK, w)
        e_q = tot3 + rel_base
        e_k = -rel_base
    else:
        rel_base = cum3
        e_q = cum3
        e_k = tot3 - cum3
    k_out = jnp.exp(lk3 + e_k).astype(BF16).reshape(t, w)
    last = slice(HG_CHUNK - 1, HG_CHUNK)
    dec = (jnp.exp(cum3[:, last, :pair_w]), jnp.exp(cum3[:, last, pair_w:]))
    vb = vr.astype(BF16)
    even = (lane_head & 1) == 0
    zb = jnp.zeros((t, w), BF16)
    k_eo = (jnp.where(even, k_out, zb).reshape(nc, HG_CHUNK, w), jnp.where(even, zb, k_out).reshape(nc, HG_CHUNK, w))
    v_eo = (jnp.where(even, vb, zb).reshape(nc, HG_CHUNK, w), jnp.where(even, zb, vb).reshape(nc, HG_CHUNK, w))

    if emit:
        q3 = _silu(qr.astype(F32)).reshape(nc, HG_CHUNK, w)
        row3 = lax.broadcasted_iota(jnp.int32, (nc, HG_CHUNK, w), 1)
        row_part = lax.broadcasted_iota(jnp.int32, (nc, half, w), 1) + (0 if rev else half)
        q_in = (q3 * jnp.exp(e_q)).astype(BF16)
        rel_key = rel_base - lk3
        acc_all = jnp.zeros((t, 2 * HG_DK), F32)
        acc_half = jnp.zeros((t // 2, 2 * HG_DK), F32)
        part = slice(0, half) if rev else slice(half, HG_CHUNK)
        for s in range(HG_CHUNK):
            whole = (s >= half) if rev else (s < half)
            rows = slice(0, HG_CHUNK) if whole else part
            rel = rel_base[:, rows, :] - rel_key[:, s:s + 1, :]
            rr = row3 if whole else row_part
            valid = (rr <= s) if rev else (rr >= s)
            wgt = q3[:, rows, :] * jnp.where(valid, jnp.exp(rel), 0.0)
            sc = jnp.dot(wgt.reshape(-1, w).astype(BF16), esel_ref[s], preferred_element_type=F32)
            if whole:
                acc_all = acc_all + sc
            else:
                acc_half = acc_half + sc
        a3 = acc_all.reshape(nc, HG_CHUNK, 2 * HG_DK)
        h3 = acc_half.reshape(nc, half, 2 * HG_DK)
        if rev:
            sc3 = jnp.concatenate([a3[:, :half] + h3, a3[:, half:]], axis=1)
        else:
            sc3 = jnp.concatenate([a3[:, :half], a3[:, half:] + h3], axis=1)
        scb = sc3.astype(BF16)
        vb3 = vb.reshape(nc, HG_CHUNK, w)
        lh3 = lane_head.reshape(nc, HG_CHUNK, w)
        zb3 = jnp.zeros((nc, HG_CHUNK, w), BF16)
        v_head = [jnp.where(lh3 == h, vb3, zb3) for h in range(HG_HEADS)]
        pad = jnp.zeros((2 * HG_DK - HG_HEADS * HG_CHUNK, w), BF16)

    tn = (((0,), (0,)), ((), ()))
    nt = (((1,), (1,)), ((), ()))
    kv0, kv1 = [], []
    for c in range(nc):
        v2 = jnp.concatenate([v_eo[0][c], v_eo[1][c]], axis=0)
        k2 = jnp.concatenate([k_eo[0][c], k_eo[1][c]], axis=0)
        kv0.append(lax.dot_general(v2[:, :pair_w], k2[:, :pair_w], tn, preferred_element_type=F32))
        kv1.append(lax.dot_general(v2[:, pair_w:], k2[:, pair_w:], tn, preferred_element_type=F32))
    kv0 = jnp.concatenate(kv0, axis=0)
    kv1 = jnp.concatenate(kv1, axis=0)

    st0, st1 = st
    outs = [None] * nc
    order = range(nc - 1, -1, -1) if rev else range(nc)
    for c in order:
        if emit:
            vmat = jnp.concatenate([vh[c] for vh in v_head] + [pad], axis=0)
            o_c = jnp.dot(scb[c], vmat, preferred_element_type=F32)
            qc = q_in[c]
            o_int = jnp.concatenate(
                [lax.dot_general(qc[:, :pair_w], st0.astype(BF16), nt, preferred_element_type=F32),
                 lax.dot_general(qc[:, pair_w:], st1.astype(BF16), nt, preferred_element_type=F32)], axis=1)
            outs[c] = o_c + o_int
        blk_rows = slice(c * pair_w, (c + 1) * pair_w)
        st0 = st0 * dec[0][c] + kv0[blk_rows]
        st1 = st1 * dec[1][c] + kv1[blk_rows]
    o = jnp.concatenate(outs, axis=0) if emit else None
    return o, (st0, st1)


def _hg_seq(p_ref, o_ref, st_ref, lb, esel_ref, emit):
    n = p_ref.shape[0] // HG_TILE
    if emit:
        o_ref[...] = jnp.zeros(o_ref.shape, F32)

    def body(j, carry):
        for d, rev in ((0, False), (1, True)):
            jt = (n - 1 - j) if rev else j
            r0 = pl.multiple_of(jt * HG_TILE, HG_TILE)
            rows = pl.ds(r0, HG_TILE)
            qr = p_ref[rows, 0:HG_W]
            zr = p_ref[rows, (1 + d) * HG_W:(2 + d) * HG_W]
            vr = p_ref[rows, 3 * HG_W:4 * HG_W]
            o, (st0, st1) = _hg_tile(qr, zr, vr, (st_ref[d, 0], st_ref[d, 1]), lb, esel_ref, rev, emit)
            st_ref[d, 0] = st0
            st_ref[d, 1] = st1
            if emit:
                o_ref[rows, :] += o
        return carry

    lax.fori_loop(0, n, body, 0, unroll=2)


def _hg_readout(p_ref, o_ref, y_ref, ng, blk):
    n = p_ref.shape[0] // HG_TILE

    def body(j, carry):
        rows = pl.ds(pl.multiple_of(j * HG_TILE, HG_TILE), HG_TILE)
        o = o_ref[rows, :]
        sq = o * o
        hi = sq.astype(BF16)
        lo = (sq - hi.astype(F32)).astype(BF16)
        ms = (jnp.dot(hi, blk, preferred_element_type=F32)
              + jnp.dot(lo, blk, preferred_element_type=F32)) * (1.0 / HG_DK)
        g = p_ref[rows, 4 * HG_W:5 * HG_W].astype(F32)
        y_ref[rows, :] = (o * lax.rsqrt(ms + EPS) * ng * _silu(g)).astype(y_ref.dtype)
        return carry

    lax.fori_loop(0, n, body, 0)


def _hg_kernel(*refs, want_ctx):
    if want_ctx:
        pl_ref, pc_ref, lb_ref, ng_ref, blk_ref, esel_ref, yl_ref, yc_ref, ol_ref, oc_ref, st_ref = refs
    else:
        pl_ref, pc_ref, lb_ref, ng_ref, blk_ref, esel_ref, yl_ref, ol_ref, st_ref = refs
        yc_ref = oc_ref = None
    lb = lb_ref[...]
    ng = ng_ref[...]
    blk = blk_ref[...]
    st_ref[...] = jnp.zeros(st_ref.shape, F32)
    _hg_seq(pc_ref, oc_ref, st_ref, lb, esel_ref, want_ctx)
    _hg_seq(pl_ref, ol_ref, st_ref, lb, esel_ref, True)
    _hg_readout(pl_ref, ol_ref, yl_ref, ng, blk)
    if want_ctx:
        _hg_readout(pc_ref, oc_ref, yc_ref, ng, blk)


def _hgrn2(p_l, p_c, lb, ng, blk, esel, *, want_ctx):
    b, l, _ = p_l.shape
    c = p_c.shape[1]
    seq = lambda n, w: pl.BlockSpec((None, n, w), lambda bb: (bb, 0, 0))
    vec = pl.BlockSpec((1, HG_W), lambda bb: (0, 0))
    out_shape = [jax.ShapeDtypeStruct((b, l, HG_W), ACT)]
    out_specs = [seq(l, HG_W)]
    scratch = [pltpu.VMEM((l, HG_W), F32)]
    if want_ctx:
        out_shape.append(jax.ShapeDtypeStruct((b, c, HG_W), ACT))
        out_specs.append(seq(c, HG_W))
        scratch.append(pltpu.VMEM((c, HG_W), F32))
    scratch.append(pltpu.VMEM((2, 2, HG_W // 2, HG_W // 2), F32))
    res = pl.pallas_call(
        functools.partial(_hg_kernel, want_ctx=want_ctx),
        out_shape=out_shape,
        grid=(b,),
        in_specs=[seq(l, IN_HG), seq(c, IN_HG), vec, vec,
                  pl.BlockSpec((HG_W, HG_W), lambda bb: (0, 0)),
                  pl.BlockSpec(esel.shape, lambda bb: (0, 0, 0))],
        out_specs=out_specs,
        scratch_shapes=scratch,
        compiler_params=_cparams("parallel"),
        name="hgrn2_mixer",
    )(p_l, p_c, lb, ng, blk, esel)
    return (res[0], res[1]) if want_ctx else (res[0], None)


def _attn_kernel(*refs, local, n_ctx, seq_len):
    if local:
        (sink_ref, q_ref, k_ref, v_ref, kc_ref, vc_ref, rep_ref, o_ref,
         k4_ref, v4_ref, kc4_ref, vc4_ref) = refs
    else:
        sink_ref, q_ref, kc_ref, vc_ref, rep_ref, o_ref, kc4_ref, vc4_ref = refs
    span = 3 * ATT_BLOCK

    rep = rep_ref[...]

    def expand(src, dst, rows):
        step = min(rows, 512)

        def body(i, carry):
            r = pl.ds(pl.multiple_of(i * step, step), step)
            dst[r, :] = jnp.dot(src[r, :], rep, preferred_element_type=F32).astype(BF16)
            return carry

        lax.fori_loop(0, rows // step, body, 0)

    expand(kc_ref, kc4_ref, n_ctx)
    expand(vc_ref, vc4_ref, n_ctx)
    if local:
        expand(k_ref, k4_ref, seq_len)
        expand(v_ref, v4_ref, seq_len)

    rows = ATT_GROUP * ATT_BLOCK
    nt = (((1,), (1,)), ((), ()))
    nb = seq_len // ATT_BLOCK

    def bias_for(offset):
        r = lax.broadcasted_iota(jnp.int32, (ATT_BLOCK, span), 0)
        c = lax.broadcasted_iota(jnp.int32, (ATT_BLOCK, span), 1)
        b = jnp.where(jnp.abs(offset + r - c) <= WINDOW, 0.0, -jnp.inf).astype(F32)
        return jnp.concatenate([b] * ATT_GROUP, axis=0)

    def block(n, start, bias):
        lane_head = lax.broadcasted_iota(jnp.int32, (ATT_BLOCK, ATT_SLAB), 1) // HEAD_DIM
        row_head = lax.broadcasted_iota(jnp.int32, (rows, 1), 0) // ATT_BLOCK
        qrows = pl.ds(pl.multiple_of(n * ATT_BLOCK, ATT_BLOCK), ATT_BLOCK)
        for h in range(ATT_KV_HEADS):
            cols = slice(h * ATT_SLAB, (h + 1) * ATT_SLAB)
            qs = q_ref[qrows, cols]
            zero = jnp.zeros_like(qs)
            lhs = jnp.concatenate([jnp.where(lane_head == g, qs, zero) for g in range(ATT_GROUP)], axis=0)
            sink = jnp.zeros((rows, 1), F32)
            for g in range(ATT_GROUP):
                sink = jnp.where(row_head == g, sink_ref[0, h * ATT_GROUP + g], sink)
            s_ctx = lax.dot_general(lhs, kc4_ref[:, cols], nt, preferred_element_type=F32)
            m = jnp.maximum(jnp.max(s_ctx, axis=-1, keepdims=True), sink)
            if local:
                s_loc = lax.dot_general(lhs, k4_ref[pl.ds(start, span), cols], nt, preferred_element_type=F32)
                s_loc = s_loc + bias
                m = jnp.maximum(m, jnp.max(s_loc, axis=-1, keepdims=True))
            e_ctx = jnp.exp(s_ctx - m)
            den = jnp.sum(e_ctx, axis=-1, keepdims=True) + jnp.exp(sink - m)
            o = jnp.dot(e_ctx.astype(BF16), vc4_ref[:, cols], preferred_element_type=F32)
            if local:
                e_loc = jnp.exp(s_loc - m)
                den = den + jnp.sum(e_loc, axis=-1, keepdims=True)
                o = o + jnp.dot(e_loc.astype(BF16), v4_ref[pl.ds(start, span), cols],
                                preferred_element_type=F32)
            o = o / den
            y = jnp.zeros((ATT_BLOCK, ATT_SLAB), F32)
            for g in range(ATT_GROUP):
                y = jnp.where(lane_head == g, o[g * ATT_BLOCK:(g + 1) * ATT_BLOCK], y)
            o_ref[qrows, cols] = y.astype(o_ref.dtype)

    if not local:
        def ctx_block(n, carry):
            block(n, None, None)
            return carry

        lax.fori_loop(0, nb, ctx_block, 0)
        return

    block(0, 0, bias_for(0))
    bias_mid = bias_for(ATT_BLOCK)

    def mid_block(n, carry):
        block(n, pl.multiple_of((n - 1) * ATT_BLOCK, ATT_BLOCK), bias_mid)
        return carry

    lax.fori_loop(1, nb - 1, mid_block, 0, unroll=5)
    block(nb - 1, seq_len - span, bias_for(2 * ATT_BLOCK))


def _attention(sink, q, k, v, kc, vc, rep, *, local):
    b, l, _ = q.shape
    c = kc.shape[1]
    whole = lambda n, w: pl.BlockSpec((None, n, w), lambda bb: (bb, 0, 0))
    smem = pl.BlockSpec(memory_space=pltpu.SMEM)
    repspec = pl.BlockSpec(rep.shape, lambda bb: (0, 0))
    wide = ATT_KV_HEADS * ATT_SLAB
    kv = ATT_KV_W
    if local:
        args = (sink, q, k, v, kc, vc, rep)
        in_specs = [smem, whole(l, ATT_W), whole(l, kv), whole(l, kv), whole(c, kv), whole(c, kv), repspec]
        scratch = [pltpu.VMEM((l, wide), BF16), pltpu.VMEM((l, wide), BF16),
                   pltpu.VMEM((c, wide), BF16), pltpu.VMEM((c, wide), BF16)]
    else:
        args = (sink, q, kc, vc, rep)
        in_specs = [smem, whole(l, ATT_W), whole(c, kv), whole(c, kv), repspec]
        scratch = [pltpu.VMEM((c, wide), BF16), pltpu.VMEM((c, wide), BF16)]
    return pl.pallas_call(
        functools.partial(_attn_kernel, local=local, n_ctx=c, seq_len=l),
        out_shape=jax.ShapeDtypeStruct((b, l, ATT_W), ACT),
        grid=(b,),
        in_specs=in_specs,
        out_specs=whole(l, ATT_W),
        scratch_shapes=scratch,
        compiler_params=_cparams("parallel"),
        name="window_attention" if local else "context_attention",
    )(*args)


def _lru_scan_tile(a, u, carry, rev):
    t, w = a.shape
    row = lax.broadcasted_iota(jnp.int32, (t, w), 0)
    sft = 1
    while sft < t:
        if sft < 8:
            ok = (row < t - sft) if rev else (row >= sft)
            shift = (t - sft) if rev else sft
            u_p = jnp.where(ok, pltpu.roll(u, shift, 0), 0.0)
            a_p = jnp.where(ok, pltpu.roll(a, shift, 0), 1.0)
        elif rev:
            u_p = jnp.concatenate([u[sft:], jnp.zeros((sft, w), F32)], axis=0)
            a_p = jnp.concatenate([a[sft:], jnp.ones((sft, w), F32)], axis=0)
        else:
            u_p = jnp.concatenate([jnp.zeros((sft, w), F32), u[:t - sft]], axis=0)
            a_p = jnp.concatenate([jnp.ones((sft, w), F32), a[:t - sft]], axis=0)
        u = u + a * u_p
        a = a * a_p
        sft *= 2
    h = u + a * carry
    new_carry = h[0:1, :] if rev else h[t - 1:t, :]
    return h, new_carry


def _lru_seq(p_ref, y_ref, xpad_ref, hsum_ref, carries, prm, emit):
    n_rows = p_ref.shape[0]
    t = min(LRU_TILE, n_rows)
    n = n_rows // t
    cw, cb, wr, br, wi, bi, sp = prm
    pad = 8
    zeros = jnp.zeros((pad, LRU_W), F32)
    xpad_ref[0:pad, :] = zeros
    xpad_ref[pad + n_rows:2 * pad + n_rows, :] = zeros

    def fill(j, carry):
        r0 = pl.multiple_of(j * t, t)
        xpad_ref[pl.ds(r0 + pad, t), :] = p_ref[pl.ds(r0, t), 0:LRU_W].astype(F32)
        if emit:
            hsum_ref[pl.ds(r0, t), :] = jnp.zeros((t, LRU_W), F32)
        return carry

    lax.fori_loop(0, n, fill, 0)
    win_rows = t + 2 * pad

    def body(j, carry):
        new = []
        for d in range(2):
            rev = d == 1
            jt = (n - 1 - j) if rev else j
            r0 = pl.multiple_of(jt * t, t)
            win = xpad_ref[pl.ds(r0, win_rows), :]
            xc = cb
            for k in range(LRU_CONV):
                shift = (LRU_CONV_LEFT - k) % win_rows
                sh = win if shift == 0 else pltpu.roll(win, shift, 0)
                xc = xc + cw[k:k + 1, :] * sh[pad:pad + t, :]
            xb = xc.astype(BF16)
            r = jax.nn.sigmoid(jnp.dot(xb, wr[d], preferred_element_type=F32) + br[d:d + 1, :])
            i = jax.nn.sigmoid(jnp.dot(xb, wi[d], preferred_element_type=F32) + bi[d:d + 1, :])
            log_a = (-LRU_C) * r * sp[d:d + 1, :]
            a = jnp.exp(log_a)
            u = jnp.sqrt(-jnp.tanh(log_a) * (a * a + 1.0)) * (i * xc)
            h, c_new = _lru_scan_tile(a, u, carry[d], rev)
            new.append(c_new)
            if emit:
                hsum_ref[pl.ds(r0, t), :] += h
        return tuple(new)

    carries = lax.fori_loop(0, n, body, carries, unroll=2)

    if emit:
        def out(j, carry):
            rows = pl.ds(pl.multiple_of(j * t, t), t)
            yv = p_ref[rows, LRU_W:2 * LRU_W].astype(F32)
            y_ref[rows, :] = (jax.nn.gelu(yv) * hsum_ref[rows, :]).astype(y_ref.dtype)
            return carry

        lax.fori_loop(0, n, out, 0)
    return carries


def _lru_kernel(*refs, want_ctx):
    if want_ctx:
        (pl_ref, pc_ref, cw_ref, cb_ref, wr_ref, br_ref, wi_ref, bi_ref, lam_ref,
         yl_ref, yc_ref, xpad_ref, hl_ref, hc_ref) = refs
    else:
        (pl_ref, pc_ref, cw_ref, cb_ref, wr_ref, br_ref, wi_ref, bi_ref, lam_ref,
         yl_ref, xpad_ref, hl_ref) = refs
        yc_ref = hc_ref = None
    sp = jax.nn.softplus(-lam_ref[...])
    prm = (cw_ref[...], cb_ref[...], (wr_ref[0], wr_ref[1]), br_ref[...],
           (wi_ref[0], wi_ref[1]), bi_ref[...], sp)
    zero = jnp.zeros((1, LRU_W), F32)
    carries = _lru_seq(pc_ref, yc_ref, xpad_ref, hc_ref, (zero, zero), prm, want_ctx)
    _lru_seq(pl_ref, yl_ref, xpad_ref, hl_ref, carries, prm, True)


def _rglru(p_l, p_c, cw, cb, wr, br, wi, bi, lam, *, want_ctx):
    b, l, _ = p_l.shape
    c = p_c.shape[1]
    seq = lambda n, w: pl.BlockSpec((None, n, w), lambda bb: (bb, 0, 0))
    full = lambda a: pl.BlockSpec(a.shape, lambda bb: (0,) * a.ndim)
    out_shape = [jax.ShapeDtypeStruct((b, l, LRU_W), ACT)]
    out_specs = [seq(l, LRU_W)]
    scratch = [pltpu.VMEM((l + 16, LRU_W), F32), pltpu.VMEM((l, LRU_W), F32)]
    if want_ctx:
        out_shape.append(jax.ShapeDtypeStruct((b, c, LRU_W), ACT))
        out_specs.append(seq(c, LRU_W))
        scratch.append(pltpu.VMEM((c, LRU_W), F32))
    params = (cw, cb, wr, br, wi, bi, lam)
    res = pl.pallas_call(
        functools.partial(_lru_kernel, want_ctx=want_ctx),
        out_shape=out_shape,
        grid=(b,),
        in_specs=[seq(l, 2 * LRU_W), seq(c, 2 * LRU_W)] + [full(a) for a in params],
        out_specs=out_specs,
        scratch_shapes=scratch,
        compiler_params=_cparams("parallel"),
        name="rglru_mixer",
    )(p_l, p_c, *params)
    return (res[0], res[1]) if want_ctx else (res[0], None)


def _outproj_kernel(hg_ref, at_ref, lr_ref, x_ref, w_ref, g1_ref, x1_ref):
    ycat = jnp.concatenate([hg_ref[...], at_ref[...], lr_ref[...]], axis=1)
    y = jnp.dot(ycat, w_ref[...], preferred_element_type=F32)
    x1_ref[...] = x_ref[...] + g1_ref[...] * y


def _outproj(hg, at, lr, x, w_out, g1, *, tm):
    b, l, d = x.shape
    tm = min(tm, l)
    row = lambda w: pl.BlockSpec((None, tm, w), lambda bb, i: (bb, i, 0))
    return pl.pallas_call(
        _outproj_kernel,
        out_shape=jax.ShapeDtypeStruct((b, l, d), F32),
        grid=(b, l // tm),
        in_specs=[row(HG_W), row(ATT_W), row(LRU_W), row(d),
                  pl.BlockSpec(w_out.shape, lambda bb, i: (0, 0)),
                  pl.BlockSpec((None, 1, d), _bmap(g1.shape[0], b))],
        out_specs=row(d),
        compiler_params=_cparams("parallel", "parallel"),
        name="outproj_residual",
    )(hg, at, lr, x, w_out, g1)


def _ffn_kernel(x_ref, xp_ref, xn_ref, ng_ref, sh_ref, sc_ref, wu_ref, cw_ref, cb_ref, wd_ref, g2_ref, fg_ref,
                o_ref, *, final_norm, d_ff):
    i = pl.program_id(1)
    nt = pl.num_programs(1)
    tm = x_ref.shape[0]
    ext = tm + 2 * HALO
    ng, sh, sc = ng_ref[...], sh_ref[...], sc_ref[...]
    h_prev = jnp.where(i > 0, _norm_mod(xp_ref[...], ng, sh, sc), 0.0)
    h_next = jnp.where(i < nt - 1, _norm_mod(xn_ref[...], ng, sh, sc), 0.0)
    hext = jnp.concatenate([h_prev, _norm_mod(x_ref[...], ng, sh, sc), h_next], axis=0).astype(BF16)
    chunks = [(c0, min(FFN_CHUNK, d_ff - c0)) for c0 in range(0, d_ff, FFN_CHUNK)]

    def up(c0, cw):
        return (jnp.dot(hext, wu_ref[:, c0:c0 + cw], preferred_element_type=F32),
                jnp.dot(hext, wu_ref[:, d_ff + c0:d_ff + c0 + cw], preferred_element_type=F32))

    def conv(u, cols):
        c = cb_ref[:, cols] + cw_ref[1:2, cols] * u[HALO:HALO + tm]
        c = c + cw_ref[0:1, cols] * pltpu.roll(u, 1, 0)[HALO:HALO + tm]
        return c + cw_ref[2:3, cols] * pltpu.roll(u, ext - 1, 0)[HALO:HALO + tm]

    acc = jnp.zeros((tm, o_ref.shape[-1]), F32)
    u_next = up(*chunks[0])
    for j, (c0, cw) in enumerate(chunks):
        ug, uv = u_next
        if j + 1 < len(chunks):
            u_next = up(*chunks[j + 1])
        act = _silu(conv(ug, slice(c0, c0 + cw))) * conv(uv, slice(d_ff + c0, d_ff + c0 + cw))
        acc = acc + jnp.dot(act.astype(BF16), wd_ref[c0:c0 + cw, :], preferred_element_type=F32)
    out = x_ref[...] + g2_ref[...] * acc
    if final_norm:
        out = out * lax.rsqrt(jnp.mean(out * out, axis=-1, keepdims=True) + EPS) * fg_ref[...]
    o_ref[...] = out


def _conv_ffn(x1, ng, shift, scale, w_up, conv_w, conv_b, w_down, g2, final_g, *, final_norm, tm):
    b, l, d = x1.shape
    tm = min(tm, l)
    d_ff = w_down.shape[0]
    hb = tm // HALO
    last = l // HALO - 1
    row = pl.BlockSpec((None, tm, d), lambda bb, i: (bb, i, 0))
    prev = pl.BlockSpec((None, HALO, d), lambda bb, i: (bb, jnp.maximum(i * hb - 1, 0), 0))
    nxt = pl.BlockSpec((None, HALO, d), lambda bb, i: (bb, jnp.minimum((i + 1) * hb, last), 0))
    const = lambda a: pl.BlockSpec(a.shape, lambda bb, i: (0,) * a.ndim, pipeline_mode=pl.Buffered(1))
    par = lambda a: pl.BlockSpec((None, 1, d), _bmap(a.shape[0], b))
    vec = pl.BlockSpec((1, d), lambda bb, i: (0, 0))
    return pl.pallas_call(
        functools.partial(_ffn_kernel, final_norm=final_norm, d_ff=d_ff),
        out_shape=jax.ShapeDtypeStruct((b, l, d), F32),
        grid=(b, l // tm),
        in_specs=[row, prev, nxt, vec, par(shift), par(scale), const(w_up), const(conv_w), const(conv_b),
                  const(w_down), par(g2), vec],
        out_specs=row,
        compiler_params=_cparams("parallel", "parallel"),
        name="conv_ffn",
    )(x1, x1, x1, ng, shift, scale, w_up, conv_w, conv_b, w_down, g2, final_g)


def _rope_tables(seq_len):
    n_freq = HEAD_DIM // 4
    inv = ROPE_BASE ** (-jnp.arange(n_freq, dtype=F32) / n_freq)
    pos = jnp.arange(seq_len)
    r = (pos // GRID_W).astype(F32)
    col = (pos % GRID_W).astype(F32)
    ang = jnp.concatenate([r[:, None] * inv, col[:, None] * inv], axis=-1)
    cos, sin = jnp.cos(ang), jnp.sin(ang)
    cos_h = jnp.concatenate([cos, cos], axis=-1)
    sin_h = jnp.concatenate([-sin, sin], axis=-1)
    return jnp.tile(cos_h, (1, ATT_KV_HEADS)), jnp.tile(sin_h, (1, ATT_KV_HEADS))


def _block_diag(w):
    n, d, e = w.shape
    eye = jnp.eye(n, dtype=w.dtype)
    return (eye[:, None, :, None] * w[:, :, None, :]).reshape(n * d, n * e)


def kernel(x, c, ctx, c_ctx, ada_w, ada_b, norm_mix_g, norm_ffn_g, w_in, hg_lb_raw, hg_norm_g, att_sink,
           lru_conv_w, lru_conv_b, lru_w_r, lru_b_r, lru_w_i, lru_b_i, lru_lambda, w_out, ffn_w_up,
           ffn_conv_w, ffn_conv_b, ffn_w_down, final_norm_g):
    b, l, d = x.shape
    n_ctx = ctx.shape[1]
    depth = w_in.shape[0]
    assert l % (GRID_W * 8) == 0 and n_ctx % HG_TILE == 0 and d == c.shape[-1]

    rows = -(-(b + 1) // 8) * 8
    crows = jnp.zeros((rows, d), F32).at[:b].set(c).at[b].set(c_ctx)
    mod = _modulation(crows, ada_w, ada_b)
    lbs = _lower_bounds(hg_lb_raw)

    cos, sin = _rope_tables(l)
    head_of = jnp.arange(HG_W) // HG_DK
    blk = (head_of[:, None] == head_of[None, :]).astype(BF16)
    col = head_of[None, :, None] * HG_CHUNK + jnp.arange(HG_CHUNK)[:, None, None]
    esel = (col == jnp.arange(2 * HG_DK)[None, None, :]).astype(BF16)
    src = jnp.arange(ATT_KV_W)
    dst = jnp.arange(ATT_KV_HEADS * ATT_SLAB)
    rep = ((src[:, None] // HEAD_DIM == dst[None, :] // ATT_SLAB)
           & (src[:, None] % HEAD_DIM == dst[None, :] % HEAD_DIM)).astype(BF16)

    h_ctx = ctx
    for layer in range(depth):
        want_ctx = layer < depth - 1
        m = mod[layer]
        lat = [m[:b, k * d:(k + 1) * d].reshape(b, 1, d) for k in range(6)]
        cx = [m[b:b + 1, k * d:(k + 1) * d].reshape(1, 1, d) for k in range(6)]
        w_in_l = w_in[layer].astype(BF16)
        ng_mix = norm_mix_g[layer].reshape(1, d)
        ng_ffn = norm_ffn_g[layer].reshape(1, d)

        hg_l, q_l, k_l, v_l, lr_l = _inproj(x, ng_mix, lat[0], lat[1], w_in_l, cos, sin, rope=True, tm=1024)
        hg_c, q_c, k_c, v_c, lr_c = _inproj(h_ctx, ng_mix, cx[0], cx[1], w_in_l, cos, sin, rope=False, tm=256)

        y_hg_l, y_hg_c = _hgrn2(hg_l, hg_c, lbs[layer].reshape(1, HG_W), hg_norm_g[layer].reshape(1, HG_W),
                                blk, esel, want_ctx=want_ctx)
        sink = att_sink[layer].reshape(1, ATT_HEADS)
        y_at_l = _attention(sink, q_l, k_l, v_l, k_c, v_c, rep, local=True)
        y_lr_l, y_lr_c = _rglru(lr_l, lr_c, lru_conv_w[layer], lru_conv_b[layer].reshape(1, LRU_W),
                                jnp.stack([_block_diag(lru_w_r[layer, dd]) for dd in range(2)]).astype(BF16),
                                lru_b_r[layer],
                                jnp.stack([_block_diag(lru_w_i[layer, dd]) for dd in range(2)]).astype(BF16),
                                lru_b_i[layer], lru_lambda[layer], want_ctx=want_ctx)

        w_out_l = w_out[layer].astype(BF16)
        w_up_l = ffn_w_up[layer].astype(BF16)
        cw_l = ffn_conv_w[layer]
        w_dn_l = ffn_w_down[layer].astype(BF16)
        cb_l = ffn_conv_b[layer].reshape(1, -1)
        fg = final_norm_g.reshape(1, d)
        x1 = _outproj(y_hg_l, y_at_l, y_lr_l, x, w_out_l, lat[2], tm=1024)
        x = _conv_ffn(x1, ng_ffn, lat[3], lat[4], w_up_l, cw_l, cb_l, w_dn_l, lat[5], fg,
                      final_norm=not want_ctx, tm=512)
        if want_ctx:
            y_at_c = _attention(sink, q_c, k_c, v_c, k_c, v_c, rep, local=False)
            c1 = _outproj(y_hg_c, y_at_c, y_lr_c, h_ctx, w_out_l, cx[2], tm=256)
            h_ctx = _conv_ffn(c1, ng_ffn, cx[3], cx[4], w_up_l, cw_l, cb_l, w_dn_l, cx[5], fg,
                              final_norm=False, tm=256)
    return x
```

```python
import functools

import jax
import jax.numpy as jnp
from jax import lax
from jax.experimental import pallas as pl
from jax.experimental.pallas import tpu as pltpu

F32 = jnp.float32
BF16 = jnp.bfloat16
ACT = BF16

EPS = 1e-6
GRID_W = 64
HEAD_DIM = 64
ROPE_BASE = 10000.0

HG_HEADS = 4
HG_DK = 64
HG_W = HG_HEADS * HG_DK
HG_CHUNK = 16
HG_TILE = 128

ATT_HEADS = 8
ATT_KV_HEADS = 2
ATT_GROUP = ATT_HEADS // ATT_KV_HEADS
ATT_W = ATT_HEADS * HEAD_DIM
ATT_KV_W = ATT_KV_HEADS * HEAD_DIM
WINDOW = 128
ATT_BLOCK = 128
ATT_SLAB = ATT_GROUP * HEAD_DIM

LRU_W = 256
LRU_BLOCKS = 4
LRU_C = 8.0
LRU_CONV = 4
LRU_CONV_LEFT = 2
LRU_TILE = 128

FFN_CONV = 3
FFN_CHUNK = 512
HALO = 8

IN_HG = 5 * HG_W
IN_W = IN_HG + ATT_W + 2 * ATT_KV_W + 2 * LRU_W

VMEM_LIMIT = 52 * 1024 * 1024


def _cparams(*sem):
    return pltpu.CompilerParams(dimension_semantics=sem, vmem_limit_bytes=VMEM_LIMIT)


def _silu(x):
    return x * jax.nn.sigmoid(x)


def _mod_kernel(c_ref, w_ref, b_ref, o_ref):
    s = _silu(c_ref[...])
    o_ref[...] = jnp.dot(s, w_ref[...], precision=lax.Precision.HIGHEST,
                         preferred_element_type=F32) + b_ref[...]


def _modulation(crows, ada_w, ada_b):
    depth, d, n = ada_w.shape
    rows = crows.shape[0]
    tn = 1536
    return pl.pallas_call(
        _mod_kernel,
        out_shape=jax.ShapeDtypeStruct((depth, rows, n), F32),
        grid=(depth, n // tn),
        in_specs=[pl.BlockSpec((rows, d), lambda l, j: (0, 0)),
                  pl.BlockSpec((None, d, tn), lambda l, j: (l, 0, j)),
                  pl.BlockSpec((None, 1, tn), lambda l, j: (l, 0, j))],
        out_specs=pl.BlockSpec((None, rows, tn), lambda l, j: (l, 0, j)),
        compiler_params=_cparams("parallel", "parallel"),
        name="modulation",
    )(crows, ada_w, ada_b.reshape(depth, 1, n))


def _lb_kernel(raw_ref, o_ref):
    raw = raw_ref[...]
    e = jnp.exp(raw - jnp.max(raw, axis=0, keepdims=True))
    p = e / jnp.sum(e, axis=0, keepdims=True)
    depth = raw.shape[0]
    acc = jnp.zeros_like(p[0:1])
    for l in range(depth):
        acc = acc + p[l:l + 1]
        o_ref[l:l + 1, :] = acc - p[0:1]


def _lower_bounds(raw):
    return pl.pallas_call(_lb_kernel, out_shape=jax.ShapeDtypeStruct(raw.shape, F32),
                          name="hg_lower_bounds")(raw.astype(F32))


def _norm_mod(x, g, shift, scale):
    y = x * lax.rsqrt(jnp.mean(x * x, axis=-1, keepdims=True) + EPS)
    return (y * g) * (1.0 + scale) + shift


def _rot_half(x):
    w = x.shape[-1]
    lane = lax.broadcasted_iota(jnp.int32, x.shape, 1) & (HEAD_DIM - 1)
    return jnp.where(lane < HEAD_DIM // 2, pltpu.roll(x, w - HEAD_DIM // 2, 1),
                     pltpu.roll(x, HEAD_DIM // 2, 1))


def _inproj_kernel(x_ref, g_ref, sh_ref, sc_ref, w_ref, cos_ref, sin_ref,
                   hg_ref, q_ref, k_ref, v_ref, lru_ref, *, rope):
    h = _norm_mod(x_ref[...], g_ref[...], sh_ref[...], sc_ref[...])
    p = jnp.dot(h.astype(BF16), w_ref[...], preferred_element_type=F32)
    o = IN_HG
    hg_ref[...] = p[:, :o].astype(ACT)
    q = p[:, o:o + ATT_W] * (HEAD_DIM ** -0.5)
    k = p[:, o + ATT_W:o + ATT_W + ATT_KV_W]
    if rope:
        cos, sin = cos_ref[...], sin_ref[...]
        cos_q = jnp.concatenate([cos] * (ATT_W // ATT_KV_W), axis=1)
        sin_q = jnp.concatenate([sin] * (ATT_W // ATT_KV_W), axis=1)
        q = q * cos_q + _rot_half(q) * sin_q
        k = k * cos + _rot_half(k) * sin
    q_ref[...] = q.astype(ACT)
    k_ref[...] = k.astype(ACT)
    o += ATT_W + ATT_KV_W
    v_ref[...] = p[:, o:o + ATT_KV_W].astype(ACT)
    o += ATT_KV_W
    lru_ref[...] = p[:, o:].astype(ACT)


def _bmap(nb_arr, b_total):
    return (lambda b, i: (b, 0, 0)) if nb_arr == b_total else (lambda b, i: (0, 0, 0))


def _inproj(x, g, shift, scale, w_in, cos, sin, *, rope, tm):
    b, l, d = x.shape
    tm = min(tm, l)
    row = lambda w: pl.BlockSpec((None, tm, w), lambda bb, i: (bb, i, 0))
    par = lambda a: pl.BlockSpec((None, 1, d), _bmap(a.shape[0], b))
    tab = pl.BlockSpec((tm, ATT_KV_W), lambda bb, i: (i, 0))
    widths = (IN_HG, ATT_W, ATT_KV_W, ATT_KV_W, 2 * LRU_W)
    return pl.pallas_call(
        functools.partial(_inproj_kernel, rope=rope),
        out_shape=[jax.ShapeDtypeStruct((b, l, w), ACT) for w in widths],
        grid=(b, l // tm),
        in_specs=[row(d), pl.BlockSpec((1, d), lambda bb, i: (0, 0)), par(shift), par(scale),
                  pl.BlockSpec(w_in.shape, lambda bb, i: (0, 0)), tab, tab],
        out_specs=[row(w) for w in widths],
        compiler_params=_cparams("parallel", "parallel"),
        name="inproj_rope" if rope else "inproj_ctx",
    )(x, g, shift, scale, w_in, cos, sin)


def _hg_tile(qr, zr, vr, st, lb, esel_ref, rev, emit):
    t, w = zr.shape
    nc = t // HG_CHUNK
    half = HG_CHUNK // 2
    pair_w = w // 2
    z = zr.astype(F32)
    f = lb + (1.0 - lb) * jax.nn.sigmoid(z)
    lf = jnp.log(f)
    lk3 = (jnp.log(1.0 - lb) - jax.nn.softplus(z)).reshape(nc, HG_CHUNK, w)
    row = lax.broadcasted_iota(jnp.int32, (t, w), 0) & (HG_CHUNK - 1)
    lane_head = lax.broadcasted_iota(jnp.int32, (t, w), 1) // HG_DK
    cum = lf
    sft = 1
    while sft < HG_CHUNK:
        cum = cum + jnp.where(row >= sft, pltpu.roll(cum, sft, 0), 0.0)
        sft *= 2
    cum3 = cum.reshape(nc, HG_CHUNK, w)
    tot3 = cum3[:, HG_CHUNK - 1:HG_CHUNK, :]
    if rev:
        rel_base = (lf - cum).reshape(nc, HG_CHUNK, w)
        e_q = tot3 + rel_base
        e_k = -rel_base
    else:
        rel_base = cum3
        e_q = cum3
        e_k = tot3 - cum3
    k_out = jnp.exp(lk3 + e_k).astype(BF16).reshape(t, w)
    last = slice(HG_CHUNK - 1, HG_CHUNK)
    dec = (jnp.exp(cum3[:, last, :pair_w]), jnp.exp(cum3[:, last, pair_w:]))
    vb = vr.astype(BF16)
    even = (lane_head & 1) == 0
    zb = jnp.zeros((t, w), BF16)
    k_eo = (jnp.where(even, k_out, zb).reshape(nc, HG_CHUNK, w), jnp.where(even, zb, k_out).reshape(nc, HG_CHUNK, w))
    v_eo = (jnp.where(even, vb, zb).reshape(nc, HG_CHUNK, w), jnp.where(even, zb, vb).reshape(nc, HG_CHUNK, w))

    if emit:
        q3 = _silu(qr.astype(F32)).reshape(nc, HG_CHUNK, w)
        row3 = lax.broadcasted_iota(jnp.int32, (nc, HG_CHUNK, w), 1)
        row_part = lax.broadcasted_iota(jnp.int32, (nc, half, w), 1) + (0 if rev else half)
        q_in = (q3 * jnp.exp(e_q)).astype(BF16)
        rel_key = rel_base - lk3
        acc_all = jnp.zeros((t, 2 * HG_DK), F32)
        acc_half = jnp.zeros((t // 2, 2 * HG_DK), F32)
        part = slice(0, half) if rev else slice(half, HG_CHUNK)
        for s in range(HG_CHUNK):
            whole = (s >= half) if rev else (s < half)
            rows = slice(0, HG_CHUNK) if whole else part
            rel = rel_base[:, rows, :] - rel_key[:, s:s + 1, :]
            rr = row3 if whole else row_part
            valid = (rr <= s) if rev else (rr >= s)
            qb = q3[:, rows, :].reshape(-1, w).astype(BF16)
            relb = jnp.where(valid, rel, -jnp.inf).reshape(-1, w).astype(BF16)
            wgt = qb * jnp.exp(relb)
            sc = jnp.dot(wgt, esel_ref[s], preferred_element_type=F32)
            if whole:
                acc_all = acc_all + sc
            else:
                acc_half = acc_half + sc
        a3 = acc_all.reshape(nc, HG_CHUNK, 2 * HG_DK)
        h3 = acc_half.reshape(nc, half, 2 * HG_DK)
        if rev:
            sc3 = jnp.concatenate([a3[:, :half] + h3, a3[:, half:]], axis=1)
        else:
            sc3 = jnp.concatenate([a3[:, :half], a3[:, half:] + h3], axis=1)
        scb = sc3.astype(BF16)
        vb3 = vb.reshape(nc, HG_CHUNK, w)
        lh3 = lane_head.reshape(nc, HG_CHUNK, w)
        zb3 = jnp.zeros((nc, HG_CHUNK, w), BF16)
        v_head = [jnp.where(lh3 == h, vb3, zb3) for h in range(HG_HEADS)]
        pad = jnp.zeros((2 * HG_DK - HG_HEADS * HG_CHUNK, w), BF16)

    tn = (((0,), (0,)), ((), ()))
    nt = (((1,), (1,)), ((), ()))
    kv0, kv1 = [], []
    for c in range(nc):
        v2 = jnp.concatenate([v_eo[0][c], v_eo[1][c]], axis=0)
        k2 = jnp.concatenate([k_eo[0][c], k_eo[1][c]], axis=0)
        kv0.append(lax.dot_general(v2[:, :pair_w], k2[:, :pair_w], tn, preferred_element_type=F32))
        kv1.append(lax.dot_general(v2[:, pair_w:], k2[:, pair_w:], tn, preferred_element_type=F32))
    kv0 = jnp.concatenate(kv0, axis=0)
    kv1 = jnp.concatenate(kv1, axis=0)

    st0, st1 = st
    outs = [None] * nc
    order = range(nc - 1, -1, -1) if rev else range(nc)
    for c in order:
        if emit:
            vmat = jnp.concatenate([vh[c] for vh in v_head] + [pad], axis=0)
            o_c = jnp.dot(scb[c], vmat, preferred_element_type=F32)
            qc = q_in[c]
            o_int = jnp.concatenate(
                [lax.dot_general(qc[:, :pair_w], st0.astype(BF16), nt, preferred_element_type=F32),
                 lax.dot_general(qc[:, pair_w:], st1.astype(BF16), nt, preferred_element_type=F32)], axis=1)
            outs[c] = o_c + o_int
        blk_rows = slice(c * pair_w, (c + 1) * pair_w)
        st0 = st0 * dec[0][c] + kv0[blk_rows]
        st1 = st1 * dec[1][c] + kv1[blk_rows]
    o = jnp.concatenate(outs, axis=0) if emit else None
    return o, (st0, st1)


def _hg_seq(p_ref, o_ref, st_ref, lb, esel_ref, emit):
    n = p_ref.shape[0] // HG_TILE
    if emit:
        o_ref[...] = jnp.zeros(o_ref.shape, F32)

    def body(j, carry):
        for d, rev in ((0, False), (1, True)):
            jt = (n - 1 - j) if rev else j
            r0 = pl.multiple_of(jt * HG_TILE, HG_TILE)
            rows = pl.ds(r0, HG_TILE)
            qr = p_ref[rows, 0:HG_W]
            zr = p_ref[rows, (1 + d) * HG_W:(2 + d) * HG_W]
            vr = p_ref[rows, 3 * HG_W:4 * HG_W]
            o, (st0, st1) = _hg_tile(qr, zr, vr, (st_ref[d, 0], st_ref[d, 1]), lb, esel_ref, rev, emit)
            st_ref[d, 0] = st0
            st_ref[d, 1] = st1
            if emit:
                o_ref[rows, :] += o
        return carry

    lax.fori_loop(0, n, body, 0, unroll=2)


def _hg_readout(p_ref, o_ref, y_ref, ng, blk):
    n = p_ref.shape[0] // HG_TILE

    def body(j, carry):
        rows = pl.ds(pl.multiple_of(j * HG_TILE, HG_TILE), HG_TILE)
        o = o_ref[rows, :]
        sq = o * o
        hi = sq.astype(BF16)
        lo = (sq - hi.astype(F32)).astype(BF16)
        ms = (jnp.dot(hi, blk, preferred_element_type=F32)
              + jnp.dot(lo, blk, preferred_element_type=F32)) * (1.0 / HG_DK)
        g = p_ref[rows, 4 * HG_W:5 * HG_W].astype(F32)
        y_ref[rows, :] = (o * lax.rsqrt(ms + EPS) * ng * _silu(g)).astype(y_ref.dtype)
        return carry

    lax.fori_loop(0, n, body, 0)


def _hg_kernel(*refs, want_ctx):
    if want_ctx:
        pl_ref, pc_ref, lb_ref, ng_ref, blk_ref, esel_ref, yl_ref, yc_ref, ol_ref, oc_ref, st_ref = refs
    else:
        pl_ref, pc_ref, lb_ref, ng_ref, blk_ref, esel_ref, yl_ref, ol_ref, st_ref = refs
        yc_ref = oc_ref = None
    lb = lb_ref[...]
    ng = ng_ref[...]
    blk = blk_ref[...]
    st_ref[...] = jnp.zeros(st_ref.shape, F32)
    _hg_seq(pc_ref, oc_ref, st_ref, lb, esel_ref, want_ctx)
    _hg_seq(pl_ref, ol_ref, st_ref, lb, esel_ref, True)
    _hg_readout(pl_ref, ol_ref, yl_ref, ng, blk)
    if want_ctx:
        _hg_readout(pc_ref, oc_ref, yc_ref, ng, blk)


def _hgrn2(p_l, p_c, lb, ng, blk, esel, *, want_ctx):
    b, l, _ = p_l.shape
    c = p_c.shape[1]
    seq = lambda n, w: pl.BlockSpec((None, n, w), lambda bb: (bb, 0, 0))
    vec = pl.BlockSpec((1, HG_W), lambda bb: (0, 0))
    out_shape = [jax.ShapeDtypeStruct((b, l, HG_W), ACT)]
    out_specs = [seq(l, HG_W)]
    scratch = [pltpu.VMEM((l, HG_W), F32)]
    if want_ctx:
        out_shape.append(jax.ShapeDtypeStruct((b, c, HG_W), ACT))
        out_specs.append(seq(c, HG_W))
        scratch.append(pltpu.VMEM((c, HG_W), F32))
    scratch.append(pltpu.VMEM((2, 2, HG_W // 2, HG_W // 2), F32))
    res = pl.pallas_call(
        functools.partial(_hg_kernel, want_ctx=want_ctx),
        out_shape=out_shape,
        grid=(b,),
        in_specs=[seq(l, IN_HG), seq(c, IN_HG), vec, vec,
                  pl.BlockSpec((HG_W, HG_W), lambda bb: (0, 0)),
                  pl.BlockSpec(esel.shape, lambda bb: (0, 0, 0))],
        out_specs=out_specs,
        scratch_shapes=scratch,
        compiler_params=_cparams("parallel"),
        name="hgrn2_mixer",
    )(p_l, p_c, lb, ng, blk, esel)
    return (res[0], res[1]) if want_ctx else (res[0], None)


def _attn_kernel(*refs, local, n_ctx, seq_len):
    if local:
        (sink_ref, q_ref, k_ref, v_ref, kc_ref, vc_ref, rep_ref, o_ref,
         k4_ref, v4_ref, kc4_ref, vc4_ref) = refs
    else:
        sink_ref, q_ref, kc_ref, vc_ref, rep_ref, o_ref, kc4_ref, vc4_ref = refs
    span = 3 * ATT_BLOCK

    rep = rep_ref[...]

    def expand(src, dst, rows):
        step = min(rows, 512)

        def body(i, carry):
            r = pl.ds(pl.multiple_of(i * step, step), step)
            dst[r, :] = jnp.dot(src[r, :], rep, preferred_element_type=F32).astype(BF16)
            return carry

        lax.fori_loop(0, rows // step, body, 0)

    expand(kc_ref, kc4_ref, n_ctx)
    expand(vc_ref, vc4_ref, n_ctx)
    if local:
        expand(k_ref, k4_ref, seq_len)
        expand(v_ref, v4_ref, seq_len)

    rows = ATT_GROUP * ATT_BLOCK
    nt = (((1,), (1,)), ((), ()))
    nb = seq_len // ATT_BLOCK

    def bias_for(offset):
        r = lax.broadcasted_iota(jnp.int32, (ATT_BLOCK, span), 0)
        c = lax.broadcasted_iota(jnp.int32, (ATT_BLOCK, span), 1)
        b = jnp.where(jnp.abs(offset + r - c) <= WINDOW, 0.0, -jnp.inf).astype(F32)
        return jnp.concatenate([b] * ATT_GROUP, axis=0)

    def block(n, start, bias):
        lane_head = lax.broadcasted_iota(jnp.int32, (ATT_BLOCK, ATT_SLAB), 1) // HEAD_DIM
        row_head = lax.broadcasted_iota(jnp.int32, (rows, 1), 0) // ATT_BLOCK
        qrows = pl.ds(pl.multiple_of(n * ATT_BLOCK, ATT_BLOCK), ATT_BLOCK)
        for h in range(ATT_KV_HEADS):
            cols = slice(h * ATT_SLAB, (h + 1) * ATT_SLAB)
            qs = q_ref[qrows, cols]
            zero = jnp.zeros_like(qs)
            lhs = jnp.concatenate([jnp.where(lane_head == g, qs, zero) for g in range(ATT_GROUP)], axis=0)
            sink = jnp.zeros((rows, 1), F32)
            for g in range(ATT_GROUP):
                sink = jnp.where(row_head == g, sink_ref[0, h * ATT_GROUP + g], sink)
            s_ctx = lax.dot_general(lhs, kc4_ref[:, cols], nt, preferred_element_type=F32)
            m = jnp.maximum(jnp.max(s_ctx, axis=-1, keepdims=True), sink)
            if local:
                s_loc = lax.dot_general(lhs, k4_ref[pl.ds(start, span), cols], nt, preferred_element_type=F32)
                s_loc = s_loc + bias
                m = jnp.maximum(m, jnp.max(s_loc, axis=-1, keepdims=True))
            e_ctx = jnp.exp(s_ctx - m)
            den = jnp.sum(e_ctx, axis=-1, keepdims=True) + jnp.exp(sink - m)
            o = jnp.dot(e_ctx.astype(BF16), vc4_ref[:, cols], preferred_element_type=F32)
            if local:
                e_loc = jnp.exp(s_loc - m)
                den = den + jnp.sum(e_loc, axis=-1, keepdims=True)
                o = o + jnp.dot(e_loc.astype(BF16), v4_ref[pl.ds(start, span), cols],
                                preferred_element_type=F32)
            o = o / den
            y = jnp.zeros((ATT_BLOCK, ATT_SLAB), F32)
            for g in range(ATT_GROUP):
                y = jnp.where(lane_head == g, o[g * ATT_BLOCK:(g + 1) * ATT_BLOCK], y)
            o_ref[qrows, cols] = y.astype(o_ref.dtype)

    if not local:
        def ctx_block(n, carry):
            block(n, None, None)
            return carry

        lax.fori_loop(0, nb, ctx_block, 0)
        return

    block(0, 0, bias_for(0))
    bias_mid = bias_for(ATT_BLOCK)

    def mid_block(n, carry):
        block(n, pl.multiple_of((n - 1) * ATT_BLOCK, ATT_BLOCK), bias_mid)
        return carry

    lax.fori_loop(1, nb - 1, mid_block, 0, unroll=5)
    block(nb - 1, seq_len - span, bias_for(2 * ATT_BLOCK))


def _attention(sink, q, k, v, kc, vc, rep, *, local):
    b, l, _ = q.shape
    c = kc.shape[1]
    whole = lambda n, w: pl.BlockSpec((None, n, w), lambda bb: (bb, 0, 0))
    smem = pl.BlockSpec(memory_space=pltpu.SMEM)
    repspec = pl.BlockSpec(rep.shape, lambda bb: (0, 0))
    wide = ATT_KV_HEADS * ATT_SLAB
    kv = ATT_KV_W
    if local:
        args = (sink, q, k, v, kc, vc, rep)
        in_specs = [smem, whole(l, ATT_W), whole(l, kv), whole(l, kv), whole(c, kv), whole(c, kv), repspec]
        scratch = [pltpu.VMEM((l, wide), BF16), pltpu.VMEM((l, wide), BF16),
                   pltpu.VMEM((c, wide), BF16), pltpu.VMEM((c, wide), BF16)]
    else:
        args = (sink, q, kc, vc, rep)
        in_specs = [smem, whole(l, ATT_W), whole(c, kv), whole(c, kv), repspec]
        scratch = [pltpu.VMEM((c, wide), BF16), pltpu.VMEM((c, wide), BF16)]
    return pl.pallas_call(
        functools.partial(_attn_kernel, local=local, n_ctx=c, seq_len=l),
        out_shape=jax.ShapeDtypeStruct((b, l, ATT_W), ACT),
        grid=(b,),
        in_specs=in_specs,
        out_specs=whole(l, ATT_W),
        scratch_shapes=scratch,
        compiler_params=_cparams("parallel"),
        name="window_attention" if local else "context_attention",
    )(*args)


def _lru_scan_tile(a, u, carry, rev):
    t, w = a.shape
    row = lax.broadcasted_iota(jnp.int32, (t, w), 0)
    sft = 1
    while sft < t:
        if sft < 8:
            ok = (row < t - sft) if rev else (row >= sft)
            shift = (t - sft) if rev else sft
            u_p = jnp.where(ok, pltpu.roll(u, shift, 0), 0.0)
            a_p = jnp.where(ok, pltpu.roll(a, shift, 0), 1.0)
        elif rev:
            u_p = jnp.concatenate([u[sft:], jnp.zeros((sft, w), F32)], axis=0)
            a_p = jnp.concatenate([a[sft:], jnp.ones((sft, w), F32)], axis=0)
        else:
            u_p = jnp.concatenate([jnp.zeros((sft, w), F32), u[:t - sft]], axis=0)
            a_p = jnp.concatenate([jnp.ones((sft, w), F32), a[:t - sft]], axis=0)
        u = u + a * u_p
        a = a * a_p
        sft *= 2
    h = u + a * carry
    new_carry = h[0:1, :] if rev else h[t - 1:t, :]
    return h, new_carry


def _lru_seq(p_ref, y_ref, xpad_ref, hsum_ref, carries, prm, emit):
    n_rows = p_ref.shape[0]
    t = min(LRU_TILE, n_rows)
    n = n_rows // t
    cw, cb, wr, br, wi, bi, sp = prm
    pad = 8
    zeros = jnp.zeros((pad, LRU_W), F32)
    xpad_ref[0:pad, :] = zeros
    xpad_ref[pad + n_rows:2 * pad + n_rows, :] = zeros

    def fill(j, carry):
        r0 = pl.multiple_of(j * t, t)
        xpad_ref[pl.ds(r0 + pad, t), :] = p_ref[pl.ds(r0, t), 0:LRU_W].astype(F32)
        if emit:
            hsum_ref[pl.ds(r0, t), :] = jnp.zeros((t, LRU_W), F32)
        return carry

    lax.fori_loop(0, n, fill, 0)
    win_rows = t + 2 * pad

    def body(j, carry):
        new = []
        for d in range(2):
            rev = d == 1
            jt = (n - 1 - j) if rev else j
            r0 = pl.multiple_of(jt * t, t)
            win = xpad_ref[pl.ds(r0, win_rows), :]
            xc = cb
            for k in range(LRU_CONV):
                shift = (LRU_CONV_LEFT - k) % win_rows
                sh = win if shift == 0 else pltpu.roll(win, shift, 0)
                xc = xc + cw[k:k + 1, :] * sh[pad:pad + t, :]
            xb = xc.astype(BF16)
            r = jax.nn.sigmoid(jnp.dot(xb, wr[d], preferred_element_type=F32) + br[d:d + 1, :])
            i = jax.nn.sigmoid(jnp.dot(xb, wi[d], preferred_element_type=F32) + bi[d:d + 1, :])
            log_a = (-LRU_C) * r * sp[d:d + 1, :]
            a = jnp.exp(log_a)
            u = jnp.sqrt(-jnp.tanh(log_a) * (a * a + 1.0)) * (i * xc)
            h, c_new = _lru_scan_tile(a, u, carry[d], rev)
            new.append(c_new)
            if emit:
                hsum_ref[pl.ds(r0, t), :] += h
        return tuple(new)

    carries = lax.fori_loop(0, n, body, carries, unroll=2)

    if emit:
        def out(j, carry):
            rows = pl.ds(pl.multiple_of(j * t, t), t)
            yv = p_ref[rows, LRU_W:2 * LRU_W].astype(F32)
            y_ref[rows, :] = (jax.nn.gelu(yv) * hsum_ref[rows, :]).astype(y_ref.dtype)
            return carry

        lax.fori_loop(0, n, out, 0)
    return carries


def _lru_kernel(*refs, want_ctx):
    if want_ctx:
        (pl_ref, pc_ref, cw_ref, cb_ref, wr_ref, br_ref, wi_ref, bi_ref, lam_ref,
         yl_ref, yc_ref, xpad_ref, hl_ref, hc_ref) = refs
    else:
        (pl_ref, pc_ref, cw_ref, cb_ref, wr_ref, br_ref, wi_ref, bi_ref, lam_ref,
         yl_ref, xpad_ref, hl_ref) = refs
        yc_ref = hc_ref = None
    sp = jax.nn.softplus(-lam_ref[...])
    prm = (cw_ref[...], cb_ref[...], (wr_ref[0], wr_ref[1]), br_ref[...],
           (wi_ref[0], wi_ref[1]), bi_ref[...], sp)
    zero = jnp.zeros((1, LRU_W), F32)
    carries = _lru_seq(pc_ref, yc_ref, xpad_ref, hc_ref, (zero, zero), prm, want_ctx)
    _lru_seq(pl_ref, yl_ref, xpad_ref, hl_ref, carries, prm, True)


def _rglru(p_l, p_c, cw, cb, wr, br, wi, bi, lam, *, want_ctx):
    b, l, _ = p_l.shape
    c = p_c.shape[1]
    seq = lambda n, w: pl.BlockSpec((None, n, w), lambda bb: (bb, 0, 0))
    full = lambda a: pl.BlockSpec(a.shape, lambda bb: (0,) * a.ndim)
    out_shape = [jax.ShapeDtypeStruct((b, l, LRU_W), ACT)]
    out_specs = [seq(l, LRU_W)]
    scratch = [pltpu.VMEM((l + 16, LRU_W), F32), pltpu.VMEM((l, LRU_W), F32)]
    if want_ctx:
        out_shape.append(jax.ShapeDtypeStruct((b, c, LRU_W), ACT))
        out_specs.append(seq(c, LRU_W))
        scratch.append(pltpu.VMEM((c, LRU_W), F32))
    params = (cw, cb, wr, br, wi, bi, lam)
    res = pl.pallas_call(
        functools.partial(_lru_kernel, want_ctx=want_ctx),
        out_shape=out_shape,
        grid=(b,),
        in_specs=[seq(l, 2 * LRU_W), seq(c, 2 * LRU_W)] + [full(a) for a in params],
        out_specs=out_specs,
        scratch_shapes=scratch,
        compiler_params=_cparams("parallel"),
        name="rglru_mixer",
    )(p_l, p_c, *params)
    return (res[0], res[1]) if want_ctx else (res[0], None)


def _outproj_kernel(hg_ref, at_ref, lr_ref, x_ref, w_ref, g1_ref, x1_ref):
    ycat = jnp.concatenate([hg_ref[...], at_ref[...], lr_ref[...]], axis=1)
    y = jnp.dot(ycat, w_ref[...], preferred_element_type=F32)
    x1_ref[...] = x_ref[...] + g1_ref[...] * y


def _outproj(hg, at, lr, x, w_out, g1, *, tm):
    b, l, d = x.shape
    tm = min(tm, l)
    row = lambda w: pl.BlockSpec((None, tm, w), lambda bb, i: (bb, i, 0))
    return pl.pallas_call(
        _outproj_kernel,
        out_shape=jax.ShapeDtypeStruct((b, l, d), F32),
        grid=(b, l // tm),
        in_specs=[row(HG_W), row(ATT_W), row(LRU_W), row(d),
                  pl.BlockSpec(w_out.shape, lambda bb, i: (0, 0)),
                  pl.BlockSpec((None, 1, d), _bmap(g1.shape[0], b))],
        out_specs=row(d),
        compiler_params=_cparams("parallel", "parallel"),
        name="outproj_residual",
    )(hg, at, lr, x, w_out, g1)


def _ffn_kernel(x_ref, xp_ref, xn_ref, ng_ref, sh_ref, sc_ref, wu_ref, cw_ref, cb_ref, wd_ref, g2_ref, fg_ref,
                o_ref, *, final_norm, d_ff):
    i = pl.program_id(1)
    nt = pl.num_programs(1)
    tm = x_ref.shape[0]
    ext = tm + 2 * HALO
    ng, sh, sc = ng_ref[...], sh_ref[...], sc_ref[...]
    h_prev = jnp.where(i > 0, _norm_mod(xp_ref[...], ng, sh, sc), 0.0)
    h_next = jnp.where(i < nt - 1, _norm_mod(xn_ref[...], ng, sh, sc), 0.0)
    hext = jnp.concatenate([h_prev, _norm_mod(x_ref[...], ng, sh, sc), h_next], axis=0).astype(BF16)
    chunks = [(c0, min(FFN_CHUNK, d_ff - c0)) for c0 in range(0, d_ff, FFN_CHUNK)]

    def up(c0, cw):
        return (jnp.dot(hext, wu_ref[:, c0:c0 + cw], preferred_element_type=F32),
                jnp.dot(hext, wu_ref[:, d_ff + c0:d_ff + c0 + cw], preferred_element_type=F32))

    def conv(u, cols):
        c = cb_ref[:, cols] + cw_ref[1:2, cols] * u[HALO:HALO + tm]
        c = c + cw_ref[0:1, cols] * pltpu.roll(u, 1, 0)[HALO:HALO + tm]
        return c + cw_ref[2:3, cols] * pltpu.roll(u, ext - 1, 0)[HALO:HALO + tm]

    acc = jnp.zeros((tm, o_ref.shape[-1]), F32)
    u_next = up(*chunks[0])
    for j, (c0, cw) in enumerate(chunks):
        ug, uv = u_next
        if j + 1 < len(chunks):
            u_next = up(*chunks[j + 1])
        act = _silu(conv(ug, slice(c0, c0 + cw))) * conv(uv, slice(d_ff + c0, d_ff + c0 + cw))
        acc = acc + jnp.dot(act.astype(BF16), wd_ref[c0:c0 + cw, :], preferred_element_type=F32)
    out = x_ref[...] + g2_ref[...] * acc
    if final_norm:
        out = out * lax.rsqrt(jnp.mean(out * out, axis=-1, keepdims=True) + EPS) * fg_ref[...]
    o_ref[...] = out


def _conv_ffn(x1, ng, shift, scale, w_up, conv_w, conv_b, w_down, g2, final_g, *, final_norm, tm):
    b, l, d = x1.shape
    tm = min(tm, l)
    d_ff = w_down.shape[0]
    hb = tm // HALO
    last = l // HALO - 1
    row = pl.BlockSpec((None, tm, d), lambda bb, i: (bb, i, 0))
    prev = pl.BlockSpec((None, HALO, d), lambda bb, i: (bb, jnp.maximum(i * hb - 1, 0), 0))
    nxt = pl.BlockSpec((None, HALO, d), lambda bb, i: (bb, jnp.minimum((i + 1) * hb, last), 0))
    const = lambda a: pl.BlockSpec(a.shape, lambda bb, i: (0,) * a.ndim, pipeline_mode=pl.Buffered(1))
    par = lambda a: pl.BlockSpec((None, 1, d), _bmap(a.shape[0], b))
    vec = pl.BlockSpec((1, d), lambda bb, i: (0, 0))
    return pl.pallas_call(
        functools.partial(_ffn_kernel, final_norm=final_norm, d_ff=d_ff),
        out_shape=jax.ShapeDtypeStruct((b, l, d), F32),
        grid=(b, l // tm),
        in_specs=[row, prev, nxt, vec, par(shift), par(scale), const(w_up), const(conv_w), const(conv_b),
                  const(w_down), par(g2), vec],
        out_specs=row,
        compiler_params=_cparams("parallel", "parallel"),
        name="conv_ffn",
    )(x1, x1, x1, ng, shift, scale, w_up, conv_w, conv_b, w_down, g2, final_g)


def _rope_tables(seq_len):
    n_freq = HEAD_DIM // 4
    inv = ROPE_BASE ** (-jnp.arange(n_freq, dtype=F32) / n_freq)
    pos = jnp.arange(seq_len)
    r = (pos // GRID_W).astype(F32)
    col = (pos % GRID_W).astype(F32)
    ang = jnp.concatenate([r[:, None] * inv, col[:, None] * inv], axis=-1)
    cos, sin = jnp.cos(ang), jnp.sin(ang)
    cos_h = jnp.concatenate([cos, cos], axis=-1)
    sin_h = jnp.concatenate([-sin, sin], axis=-1)
    return jnp.tile(cos_h, (1, ATT_KV_HEADS)), jnp.tile(sin_h, (1, ATT_KV_HEADS))


def _block_diag(w):
    n, d, e = w.shape
    eye = jnp.eye(n, dtype=w.dtype)
    return (eye[:, None, :, None] * w[:, :, None, :]).reshape(n * d, n * e)


def kernel(x, c, ctx, c_ctx, ada_w, ada_b, norm_mix_g, norm_ffn_g, w_in, hg_lb_raw, hg_norm_g, att_sink,
           lru_conv_w, lru_conv_b, lru_w_r, lru_b_r, lru_w_i, lru_b_i, lru_lambda, w_out, ffn_w_up,
           ffn_conv_w, ffn_conv_b, ffn_w_down, final_norm_g):
    b, l, d = x.shape
    n_ctx = ctx.shape[1]
    depth = w_in.shape[0]
    assert l % (GRID_W * 8) == 0 and n_ctx % HG_TILE == 0 and d == c.shape[-1]

    rows = -(-(b + 1) // 8) * 8
    crows = jnp.zeros((rows, d), F32).at[:b].set(c).at[b].set(c_ctx)
    mod = _modulation(crows, ada_w, ada_b)
    lbs = _lower_bounds(hg_lb_raw)

    cos, sin = _rope_tables(l)
    head_of = jnp.arange(HG_W) // HG_DK
    blk = (head_of[:, None] == head_of[None, :]).astype(BF16)
    col = head_of[None, :, None] * HG_CHUNK + jnp.arange(HG_CHUNK)[:, None, None]
    esel = (col == jnp.arange(2 * HG_DK)[None, None, :]).astype(BF16)
    src = jnp.arange(ATT_KV_W)
    dst = jnp.arange(ATT_KV_HEADS * ATT_SLAB)
    rep = ((src[:, None] // HEAD_DIM == dst[None, :] // ATT_SLAB)
           & (src[:, None] % HEAD_DIM == dst[None, :] % HEAD_DIM)).astype(BF16)

    h_ctx = ctx
    for layer in range(depth):
        want_ctx = layer < depth - 1
        m = mod[layer]
        lat = [m[:b, k * d:(k + 1) * d].reshape(b, 1, d) for k in range(6)]
        cx = [m[b:b + 1, k * d:(k + 1) * d].reshape(1, 1, d) for k in range(6)]
        w_in_l = w_in[layer].astype(BF16)
        ng_mix = norm_mix_g[layer].reshape(1, d)
        ng_ffn = norm_ffn_g[layer].reshape(1, d)

        hg_l, q_l, k_l, v_l, lr_l = _inproj(x, ng_mix, lat[0], lat[1], w_in_l, cos, sin, rope=True, tm=1024)
        hg_c, q_c, k_c, v_c, lr_c = _inproj(h_ctx, ng_mix, cx[0], cx[1], w_in_l, cos, sin, rope=False, tm=256)

        y_hg_l, y_hg_c = _hgrn2(hg_l, hg_c, lbs[layer].reshape(1, HG_W), hg_norm_g[layer].reshape(1, HG_W),
                                blk, esel, want_ctx=want_ctx)
        sink = att_sink[layer].reshape(1, ATT_HEADS)
        y_at_l = _attention(sink, q_l, k_l, v_l, k_c, v_c, rep, local=True)
        y_lr_l, y_lr_c = _rglru(lr_l, lr_c, lru_conv_w[layer], lru_conv_b[layer].reshape(1, LRU_W),
                                jnp.stack([_block_diag(lru_w_r[layer, dd]) for dd in range(2)]).astype(BF16),
                                lru_b_r[layer],
                                jnp.stack([_block_diag(lru_w_i[layer, dd]) for dd in range(2)]).astype(BF16),
                                lru_b_i[layer], lru_lambda[layer], want_ctx=want_ctx)

        w_out_l = w_out[layer].astype(BF16)
        w_up_l = ffn_w_up[layer].astype(BF16)
        cw_l = ffn_conv_w[layer]
        w_dn_l = ffn_w_down[layer].astype(BF16)
        cb_l = ffn_conv_b[layer].reshape(1, -1)
        fg = final_norm_g.reshape(1, d)
        x1 = _outproj(y_hg_l, y_at_l, y_lr_l, x, w_out_l, lat[2], tm=1024)
        x = _conv_ffn(x1, ng_ffn, lat[3], lat[4], w_up_l, cw_l, cb_l, w_dn_l, lat[5], fg,
                      final_norm=not want_ctx, tm=512)
        if want_ctx:
            y_at_c = _attention(sink, q_c, k_c, v_c, k_c, v_c, rep, local=False)
            c1 = _outproj(y_hg_c, y_at_c, y_lr_c, h_ctx, w_out_l, cx[2], tm=256)
            h_ctx = _conv_ffn(c1, ng_ffn, cx[3], cx[4], w_up_l, cw_l, cb_l, w_dn_l, cx[5], fg,
                              final_norm=False, tm=256)
    return x
```
